```python
import jax
import jax.numpy as jnp
from jax import lax
import numpy as np

D_MODEL = 1024
BATCH = 2
SEQ = 16384
DEPTH = 2

CTX_LEN = 256
GRID_W = 64
HG_HEADS = 4
HG_DK = 128
HG_DV = 128
HG_WIDTH = HG_HEADS * HG_DK
HG_CHUNK = 64
RG_WIDTH = 512
RG_BLOCKS = 8
RG_BLOCK = RG_WIDTH // RG_BLOCKS
RG_CONV = 4
RG_C = 8.0
MIX_WIDTH = HG_WIDTH + RG_WIDTH
IN_COLS = 5 * HG_WIDTH + 2 * RG_WIDTH
D_FF = -(-8 * D_MODEL // (3 * 256)) * 256
N_MOD = 6
EPS = 1e-6

kernel_name = 'hybrid_hgrn2_rglru_prefix_dit_block'


def rms_norm(x, w):
    xf = x.astype(jnp.float32)
    y = xf * lax.rsqrt(jnp.mean(xf * xf, axis=-1, keepdims=True) + EPS)
    return (y * w.astype(jnp.float32)).astype(x.dtype)


def split_heads(a):
    b, t, _ = a.shape
    return a.reshape(b, t, HG_HEADS, -1).transpose(0, 2, 1, 3)


def merge_heads(a):
    b, h, t, d = a.shape
    return a.transpose(0, 2, 1, 3).reshape(b, t, h * d)


def flip_time(a, reverse, axis):
    return jnp.flip(a, axis=axis) if reverse else a


def hgrn2_gates(z, lb):
    lb = lb[None, :, None, :]
    log_f = jnp.log(lb + (1.0 - lb) * jax.nn.sigmoid(z))
    k = (1.0 - lb) * jax.nn.sigmoid(-z)
    return log_f, k


def hgrn2_chunk_scan(q, k, v, log_f, s0):
    b, h, t, _ = q.shape
    n = t // HG_CHUNK

    def chunks(a):
        return jnp.moveaxis(a.reshape(b, h, n, HG_CHUNK, a.shape[-1]), 2, 0)

    lower_tri = jnp.tril(jnp.ones((HG_CHUNK, HG_CHUNK), dtype=bool))[:, :, None]

    def step(s, inp):
        qc, kc, vc, lc = inp
        cum = jnp.cumsum(lc, axis=2)
        diff = cum[:, :, :, None, :] - cum[:, :, None, :, :]
        decay = jnp.exp(jnp.where(lower_tri, diff, -jnp.inf))
        scores = jnp.einsum('bhtk,bhtsk,bhsk->bhts', qc, decay, kc)
        o = (jnp.einsum('bhts,bhsv->bhtv', scores, vc)
             + jnp.einsum('bhtk,bhkv->bhtv', qc * jnp.exp(cum), s))
        total = cum[:, :, -1:, :]
        s_new = (jnp.exp(total[:, :, 0, :, None]) * s
                 + jnp.einsum('bhsk,bhsv->bhkv', kc * jnp.exp(total - cum), vc))
        return s_new, o

    s_fin, o = lax.scan(step, s0, (chunks(q), chunks(k), chunks(v), chunks(log_f)))
    return jnp.moveaxis(o, 0, 2).reshape(b, h, t, -1), s_fin


def hgrn2_mixer(p_ctx, p_lat, lb, g_norm_w, with_ctx_out):
    def parts(p):
        q, zf, zb, v, g = jnp.split(p[..., :5 * HG_WIDTH].astype(jnp.float32), 5, axis=-1)
        return jax.nn.silu(split_heads(q)), (split_heads(zf), split_heads(zb)), split_heads(v), g

    q_c, z_c, v_c, g_c = parts(p_ctx)
    q_l, z_l, v_l, g_l = parts(p_lat)
    s0 = jnp.zeros((q_c.shape[0], HG_HEADS, HG_DK, HG_DV), jnp.float32)
    outs_c, outs_l = [], []
    for d in range(2):
        rev = d == 1
        lb_d = lb[d].astype(jnp.float32).reshape(HG_HEADS, HG_DK)
        lf_c, k_c = hgrn2_gates(z_c[d], lb_d)
        lf_l, k_l = hgrn2_gates(z_l[d], lb_d)
        o_c, s_ctx = hgrn2_chunk_scan(*(flip_time(a, rev, 2) for a in (q_c, k_c, v_c, lf_c)), s0)
        o_l, _ = hgrn2_chunk_scan(*(flip_time(a, rev, 2) for a in (q_l, k_l, v_l, lf_l)), s_ctx)
        outs_c.append(flip_time(o_c, rev, 2))
        outs_l.append(flip_time(o_l, rev, 2))

    def readout(o, g):
        return merge_heads(rms_norm(o, g_norm_w)) * jax.nn.silu(g)

    out_l = readout(outs_l[0] + outs_l[1], g_l).astype(p_lat.dtype)
    out_c = readout(outs_c[0] + outs_c[1], g_c).astype(p_ctx.dtype) if with_ctx_out else None
    return out_c, out_l


def depthwise_conv(u, w, bias):
    pad_l = RG_CONV // 2
    out = lax.conv_general_dilated(
        u, w[:, None, :], window_strides=(1,), padding=[(pad_l, RG_CONV - 1 - pad_l)],
        dimension_numbers=('NWC', 'WIO', 'NWC'), feature_group_count=u.shape[-1])
    return out + bias


def rglru_coeffs(u, w_a, b_a, w_x, b_x, lam):
    ub = u.reshape(*u.shape[:-1], RG_BLOCKS, RG_BLOCK)
    r = jax.nn.sigmoid(jnp.einsum('btni,nij->btnj', ub, w_a).reshape(u.shape) + b_a)
    i = jax.nn.sigmoid(jnp.einsum('btni,nij->btnj', ub, w_x).reshape(u.shape) + b_x)
    log_a = RG_C * r * jax.nn.log_sigmoid(lam)
    return jnp.exp(log_a), jnp.sqrt(-jnp.expm1(2.0 * log_a)) * (i * u)


def linear_recurrence(a, bx, h0):
    def combine(e1, e2):
        return e1[0] * e2[0], e2[0] * e1[1] + e2[1]
    a_cum, h = lax.associative_scan(combine, (a, bx), axis=1)
    return h + a_cum * h0[:, None, :]


def rglru_mixer(p_ctx, p_lat, conv_w, conv_b, w_a, b_a, w_x, b_x, lam, with_ctx_out):
    f32 = jnp.float32
    conv_w, conv_b = conv_w.astype(f32), conv_b.astype(f32)
    u_c, gate_c = jnp.split(p_ctx[..., 5 * HG_WIDTH:].astype(f32), 2, axis=-1)
    u_l, gate_l = jnp.split(p_lat[..., 5 * HG_WIDTH:].astype(f32), 2, axis=-1)
    b, n_lat, ch = u_l.shape
    rows = n_lat // GRID_W
    u_c = depthwise_conv(u_c, conv_w, conv_b)
    u_l = depthwise_conv(u_l.reshape(b * rows, GRID_W, ch), conv_w, conv_b).reshape(b, n_lat, ch)
    h0 = jnp.zeros((b, ch), f32)
    hs_c, hs_l = [], []
    for d in range(2):
        rev = d == 1
        prm = (w_a[d].astype(f32), b_a[d].astype(f32), w_x[d].astype(f32), b_x[d].astype(f32), lam[d].astype(f32))
        a_c, x_c = rglru_coeffs(flip_time(u_c, rev, 1), *prm)
        a_l, x_l = rglru_coeffs(flip_time(u_l, rev, 1), *prm)
        h_c = linear_recurrence(a_c, x_c, h0)
        h_l = linear_recurrence(a_l, x_l, h_c[:, -1])
        hs_c.append(flip_time(h_c, rev, 1))
        hs_l.append(flip_time(h_l, rev, 1))
    out_l = ((hs_l[0] + hs_l[1]) * jax.nn.gelu(gate_l)).astype(p_lat.dtype)
    out_c = ((hs_c[0] + hs_c[1]) * jax.nn.gelu(gate_c)).astype(p_ctx.dtype) if with_ctx_out else None
    return out_c, out_l


def swiglu(h, w_gate, w_up, w_down):
    return (jax.nn.silu(h @ w_gate) * (h @ w_up)) @ w_down


def setup_inputs(seed: int = 0) -> dict:
    key = jax.random.key(seed)
    ks = jax.random.split(key, 24)

    def nrm(k, shape, scale):
        return scale * jax.random.normal(k, shape, jnp.float32)

    a0 = jax.random.uniform(ks[16], (DEPTH, 2, RG_WIDTH), jnp.float32, 0.9, 0.999)
    s = a0 ** (1.0 / RG_C)
    return {
        'x': nrm(ks[0], (BATCH, SEQ, D_MODEL), 1.0),
        'c': nrm(ks[1], (BATCH, D_MODEL), 1.0),
        'ctx': nrm(ks[2], (BATCH, CTX_LEN, D_MODEL), 1.0),
        'c_ctx': nrm(ks[3], (D_MODEL,), 1.0),
        'w_mod': nrm(ks[4], (DEPTH, D_MODEL, N_MOD * D_MODEL), 0.5 * D_MODEL ** -0.5),
        'b_mod': nrm(ks[5], (DEPTH, N_MOD * D_MODEL), 0.02),
        'norm_g': 1.0 + nrm(ks[6], (DEPTH, 4, D_MODEL), 0.02),
        'w_in': nrm(ks[7], (DEPTH, D_MODEL, IN_COLS), D_MODEL ** -0.5),
        'hg_lb_logits': nrm(ks[8], (DEPTH, 2, HG_WIDTH), 1.0),
        'hg_gnorm': 1.0 + nrm(ks[9], (DEPTH, HG_DV), 0.02),
        'rg_conv_w': nrm(ks[10], (DEPTH, RG_CONV, RG_WIDTH), RG_CONV ** -0.5),
        'rg_conv_b': nrm(ks[11], (DEPTH, RG_WIDTH), 0.02),
        'rg_w_a': nrm(ks[12], (DEPTH, 2, RG_BLOCKS, RG_BLOCK, RG_BLOCK), RG_BLOCK ** -0.5),
        'rg_b_a': nrm(ks[13], (DEPTH, 2, RG_WIDTH), 0.02),
        'rg_w_x': nrm(ks[14], (DEPTH, 2, RG_BLOCKS, RG_BLOCK, RG_BLOCK), RG_BLOCK ** -0.5),
        'rg_b_x': nrm(ks[15], (DEPTH, 2, RG_WIDTH), 0.02),
        'rg_lambda': jnp.log(s) - jnp.log1p(-s),
        'w_out': nrm(ks[17], (DEPTH, MIX_WIDTH, D_MODEL), MIX_WIDTH ** -0.5),
        'w_ffn_gate': nrm(ks[18], (DEPTH, D_MODEL, D_FF), D_MODEL ** -0.5),
        'w_ffn_up': nrm(ks[19], (DEPTH, D_MODEL, D_FF), D_MODEL ** -0.5),
        'w_ffn_down': nrm(ks[20], (DEPTH, D_FF, D_MODEL), D_FF ** -0.5),
    }


def reference(x, c, ctx, c_ctx, w_mod, b_mod, norm_g, w_in, hg_lb_logits, hg_gnorm,
              rg_conv_w, rg_conv_b, rg_w_a, rg_b_a, rg_w_x, rg_b_x, rg_lambda,
              w_out, w_ffn_gate, w_ffn_up, w_ffn_down):
    b = x.shape[0]
    lb_all = jnp.cumsum(jax.nn.softmax(hg_lb_logits.astype(jnp.float32), axis=0), axis=0)
    lb_all = lb_all - lb_all[0]
    x_lat, x_ctx = x, ctx
    for layer in range(DEPTH):
        with_ctx = layer < DEPTH - 1
        m_l = (jax.nn.silu(c) @ w_mod[layer] + b_mod[layer]).reshape(b, N_MOD, 1, D_MODEL)
        m_c = (jax.nn.silu(c_ctx) @ w_mod[layer] + b_mod[layer]).reshape(N_MOD, 1, 1, D_MODEL)
        g = norm_g[layer]
        h_l = rms_norm(x_lat, g[0]) * (1.0 + m_l[:, 1]) + m_l[:, 0]
        h_c = rms_norm(x_ctx, g[0]) * (1.0 + m_c[1]) + m_c[0]
        p_l = h_l @ w_in[layer]
        p_c = h_c @ w_in[layer]
        hg_c, hg_l = hgrn2_mixer(p_c, p_l, lb_all[layer], hg_gnorm[layer], with_ctx)
        rg_c, rg_l = rglru_mixer(p_c, p_l, rg_conv_w[layer], rg_conv_b[layer], rg_w_a[layer], rg_b_a[layer],
                                 rg_w_x[layer], rg_b_x[layer], rg_lambda[layer], with_ctx)
        o_l = jnp.concatenate([hg_l, rg_l], axis=-1) @ w_out[layer]
        x_lat = x_lat + m_l[:, 2] * rms_norm(o_l, g[1])
        f_l = rms_norm(x_lat, g[2]) * (1.0 + m_l[:, 4]) + m_l[:, 3]
        x_lat = x_lat + m_l[:, 5] * rms_norm(swiglu(f_l, w_ffn_gate[layer], w_ffn_up[layer], w_ffn_down[layer]), g[3])
        if with_ctx:
            o_c = jnp.concatenate([hg_c, rg_c], axis=-1) @ w_out[layer]
            x_ctx = x_ctx + m_c[2] * rms_norm(o_c, g[1])
            f_c = rms_norm(x_ctx, g[2]) * (1.0 + m_c[4]) + m_c[3]
            x_ctx = x_ctx + m_c[5] * rms_norm(swiglu(f_c, w_ffn_gate[layer], w_ffn_up[layer], w_ffn_down[layer]), g[3])
    return x_lat
```

```python
import functools

import numpy as np
import jax
import jax.numpy as jnp
from jax import lax
from jax.experimental import pallas as pl
from jax.experimental.pallas import tpu as pltpu

F32 = jnp.float32
BF16 = jnp.bfloat16

EPS = 1e-6
N_MOD = 6
HG_HEADS = 4
HG_DK = 128
HG_WIDTH = HG_HEADS * HG_DK
RG_WIDTH = 512
RG_BLOCKS = 8
RG_C = 8.0
GRID_W = 64
CHUNK = 128
LEVELS = (1, 2, 4, 8, 16, 32, 64)
SUBLANES = 8
VMEM_LIMIT = 56 * 1024 * 1024

COL_Q, COL_ZF, COL_ZB, COL_V, COL_G, COL_U, COL_GATE = range(7)


def _params(sem):
    return pltpu.CompilerParams(dimension_semantics=sem, vmem_limit_bytes=VMEM_LIMIT)


def _const_spec(shape):
    nd = len(shape)
    return pl.BlockSpec(shape, lambda *_: (0,) * nd)


def _sigmoid(x):
    return 1.0 / (1.0 + jnp.exp(-x))


def _silu(x):
    return x * _sigmoid(x)


def _rms(x, g):
    return x * lax.rsqrt(jnp.mean(x * x, axis=-1, keepdims=True) + EPS) * g


def _mod_kernel(c_ref, w_ref, b_ref, o_ref):
    a = _silu(c_ref[...])
    o_ref[0] = jnp.dot(a, w_ref[0], precision=lax.Precision.HIGHEST,
                       preferred_element_type=F32) + b_ref[0]


def _modulation(cvec, w_mod, b_mod):
    depth, d, n = w_mod.shape
    rows = cvec.shape[0]
    nb = 4
    bn = n // nb
    return pl.pallas_call(
        _mod_kernel,
        grid=(depth, nb),
        in_specs=[
            pl.BlockSpec((rows, d), lambda l, j: (0, 0)),
            pl.BlockSpec((1, d, bn), lambda l, j: (l, 0, j)),
            pl.BlockSpec((1, 1, bn), lambda l, j: (l, 0, j)),
        ],
        out_specs=pl.BlockSpec((1, rows, bn), lambda l, j: (l, 0, j)),
        out_shape=jax.ShapeDtypeStruct((depth, rows, n), F32),
        compiler_params=_params(("parallel", "parallel")),
        name="adaln_mod",
    )(cvec, w_mod, b_mod.reshape(depth, 1, n))


def _proj_kernel(x_ref, m_ref, g_ref, w_ref, o_ref):
    h = _rms(x_ref[0], g_ref[...]) * (1.0 + m_ref[0, 1:2, :]) + m_ref[0, 0:1, :]
    o_ref[0] = jnp.dot(h.astype(BF16), w_ref[...], preferred_element_type=F32)


def _in_proj(x, m, g0, w_in_bf, tm):
    b, t, d = x.shape
    n = w_in_bf.shape[1]
    mb = m.shape[0]
    return pl.pallas_call(
        _proj_kernel,
        grid=(b, t // tm),
        in_specs=[
            pl.BlockSpec((1, tm, d), lambda i, j: (i, j, 0)),
            pl.BlockSpec((1, N_MOD, d), lambda i, j: (i if mb > 1 else 0, 0, 0)),
            _const_spec((1, d)),
            pl.BlockSpec((d, n), lambda i, j: (0, 0), pipeline_mode=pl.Buffered(1)),
        ],
        out_specs=pl.BlockSpec((1, tm, n), lambda i, j: (i, j, 0)),
        out_shape=jax.ShapeDtypeStruct((b, t, n), F32),
        compiler_params=_params(("parallel", "parallel")),
        name="in_proj",
    )(x, m, g0.reshape(1, d), w_in_bf)


def _level_masks(reverse):
    t = np.arange(CHUNK)[:, None]
    s = np.arange(CHUNK)[None, :]
    if reverse:
        t, s = s, t
    masks = [t == s]
    for b in LEVELS:
        masks.append((t // b == s // b + 1) & ((s // b) % 2 == 0))
    return np.stack(masks).astype(np.float32)


def _hg_chunk(q_raw, z, v, lb, st_ref, mask_ref, reverse):
    n = CHUNK
    w = q_raw.shape[1]
    q = _silu(q_raw)
    e = jnp.exp(-jnp.abs(z))
    r = 1.0 / (1.0 + e)
    pos = z >= 0
    sig = jnp.where(pos, r, e * r)
    sig_neg = jnp.where(pos, e * r, r)
    k = (1.0 - lb) * sig_neg
    lf = jnp.log(lb + (1.0 - lb) * sig)

    row = lax.broadcasted_iota(jnp.int32, (n, w), 0)

    def shift(x, d):
        return pltpu.roll(x, (n - d) if reverse else d, 0)

    def unshift(x, d):
        return pltpu.roll(x, d if reverse else (n - d), 0)

    def bit(b):
        on = (row & b) != 0
        return jnp.logical_not(on) if reverse else on

    pos_in_scan = (n - 1 - row) if reverse else row
    lc = lf
    d = 1
    while d < n:
        lc = lc + jnp.where(pos_in_scan >= d, shift(lc, d), 0.0)
        d *= 2

    zb = lc
    acc = [None] * HG_HEADS
    qb = q.astype(BF16)
    kb = k.astype(BF16)
    vb = v.astype(BF16)

    def add_scores(lvl, a_bf, b_bf):
        for h in range(HG_HEADS):
            sl = slice(h * HG_DK, (h + 1) * HG_DK)
            sc = lax.dot_general(a_bf[:, sl], b_bf[:, sl], (((1,), (1,)), ((), ())),
                                 preferred_element_type=F32) * mask_ref[lvl]
            acc[h] = sc if acc[h] is None else acc[h] + sc

    add_scores(0, qb, kb)
    for lvl, b in enumerate(LEVELS):
        second = bit(b)
        anchor = jnp.where(second, shift(zb, b), zb)
        wgt = jnp.exp(-jnp.abs(lc - anchor))
        xb = (jnp.where(second, q, k) * wgt).astype(BF16)
        add_scores(lvl + 1, xb, xb)
        zb = jnp.where(second, zb, unshift(zb, b))

    q_in = (q * jnp.exp(lc)).astype(BF16)
    k_out = (k * jnp.exp(-jnp.abs(lc - zb))).astype(BF16)
    total = jnp.exp(zb[0:1, :])
    outs = []
    for h in range(HG_HEADS):
        sl = slice(h * HG_DK, (h + 1) * HG_DK)
        st = st_ref[h]
        o = jnp.dot(acc[h].astype(BF16), vb[:, sl], preferred_element_type=F32)
        o = o + lax.dot_general(q_in[:, sl], st.astype(BF16), (((1,), (1,)), ((), ())),
                                preferred_element_type=F32)
        st_ref[h] = st * total[:, sl] + lax.dot_general(
            vb[:, sl], k_out[:, sl], (((0,), (0,)), ((), ())), preferred_element_type=F32)
        outs.append(o)
    return jnp.concatenate(outs, axis=1)


def _hg_kernel(*refs, reverse, readout, n_chunks):
    if readout:
        (q_ref, z_ref, v_ref, lb_ref, s0_ref, mask_ref, of_ref, g_ref, gn_ref,
         o_ref, sout_ref, st_ref) = refs
    else:
        q_ref, z_ref, v_ref, lb_ref, s0_ref, mask_ref, o_ref, sout_ref, st_ref = refs
    j = pl.program_id(1)

    @pl.when(j == 0)
    def _():
        st_ref[...] = s0_ref[0]

    lb = lb_ref[...]

    def body(ci, carry):
        c = (n_chunks - 1 - ci) if reverse else ci
        rows = pl.ds(pl.multiple_of(c * CHUNK, CHUNK), CHUNK)
        o = _hg_chunk(q_ref[0, rows, :], z_ref[0, rows, :], v_ref[0, rows, :], lb,
                      st_ref, mask_ref, reverse)
        if readout:
            o = o + of_ref[0, rows, :]
            g = g_ref[0, rows, :]
            parts = []
            for h in range(HG_HEADS):
                sl = slice(h * HG_DK, (h + 1) * HG_DK)
                parts.append(_rms(o[:, sl], gn_ref[...]))
            o = jnp.concatenate(parts, axis=1) * _silu(g)
        o_ref[0, rows, :] = o
        return carry

    lax.fori_loop(0, n_chunks, body, 0)

    @pl.when(j == pl.num_programs(1) - 1)
    def _():
        sout_ref[0] = st_ref[...]


def _hg_scan(p, lb, s0, reverse, tb, of=None, gnorm=None):
    b, t, _ = p.shape
    nb = t // tb
    readout = of is not None
    masks = jnp.asarray(_level_masks(reverse))

    def blk(j):
        return (nb - 1 - j) if reverse else j

    def col(cidx):
        return pl.BlockSpec((1, tb, HG_WIDTH), lambda i, j: (i, blk(j), cidx))

    state_spec = pl.BlockSpec((1, HG_HEADS, HG_DK, HG_DK), lambda i, j: (i, 0, 0, 0))
    in_specs = [col(COL_Q), col(COL_ZB if reverse else COL_ZF), col(COL_V),
                _const_spec((1, HG_WIDTH)), state_spec,
                _const_spec((len(LEVELS) + 1, CHUNK, CHUNK))]
    args = [p, p, p, lb.reshape(1, HG_WIDTH), s0, masks]
    if readout:
        in_specs += [pl.BlockSpec((1, tb, HG_WIDTH), lambda i, j: (i, blk(j), 0)),
                     col(COL_G), _const_spec((1, HG_DK))]
        args += [of, p, gnorm.reshape(1, HG_DK)]
    return pl.pallas_call(
        functools.partial(_hg_kernel, reverse=reverse, readout=readout,
                          n_chunks=tb // CHUNK),
        grid=(b, nb),
        in_specs=in_specs,
        out_specs=[pl.BlockSpec((1, tb, HG_WIDTH), lambda i, j: (i, blk(j), 0)), state_spec],
        out_shape=[jax.ShapeDtypeStruct((b, t, HG_WIDTH), F32),
                   jax.ShapeDtypeStruct((b, HG_HEADS, HG_DK, HG_DK), F32)],
        scratch_shapes=[pltpu.VMEM((HG_HEADS, HG_DK, HG_DK), F32)],
        compiler_params=_params(("parallel", "arbitrary")),
        name="hgrn2_bwd" if reverse else "hgrn2_fwd",
    )(*args)


def _gelu_tanh(x):
    return 0.5 * x * (1.0 + jnp.tanh(0.7978845608028654 * (x + 0.044715 * (x * x * x))))


def _rg_kernel(*refs, reverse, row_conv, tb):
    if reverse:
        (u_ref, cw_ref, cb_ref, wg_ref, bg_ref, ls_ref, h0_ref, hf_ref, gate_ref,
         o_ref, hout_ref, a_ref, x_ref, h_ref) = refs
    else:
        (u_ref, cw_ref, cb_ref, wg_ref, bg_ref, ls_ref, h0_ref,
         o_ref, hout_ref, a_ref, x_ref, h_ref) = refs
    j = pl.program_id(1)

    @pl.when(j == 0)
    def _():
        h_ref[...] = h0_ref[0]

    u = u_ref[0]
    w = u.shape[1]
    row = lax.broadcasted_iota(jnp.int32, (tb, w), 0)
    pos = (row & (GRID_W - 1)) if row_conv else row
    last = (GRID_W if row_conv else tb) - 1
    uc = cw_ref[2:3, :] * u + cb_ref[...]
    uc = uc + cw_ref[0:1, :] * jnp.where(pos >= 2, pltpu.roll(u, 2, 0), 0.0)
    uc = uc + cw_ref[1:2, :] * jnp.where(pos >= 1, pltpu.roll(u, 1, 0), 0.0)
    uc = uc + cw_ref[3:4, :] * jnp.where(pos < last, pltpu.roll(u, tb - 1, 0), 0.0)

    gates = jnp.dot(uc.astype(BF16), wg_ref[...], preferred_element_type=F32) + bg_ref[...]
    r = _sigmoid(gates[:, :w])
    i = _sigmoid(gates[:, w:])
    a = jnp.exp((RG_C * r) * ls_ref[...])
    x = jnp.sqrt(1.0 - a * a) * (i * uc)

    sub = row & (SUBLANES - 1)
    d = 1
    while d < SUBLANES:
        if reverse:
            ok = sub <= SUBLANES - 1 - d
            xs, as_ = pltpu.roll(x, tb - d, 0), pltpu.roll(a, tb - d, 0)
        else:
            ok = sub >= d
            xs, as_ = pltpu.roll(x, d, 0), pltpu.roll(a, d, 0)
        x = jnp.where(ok, x + a * xs, x)
        a = jnp.where(ok, a * as_, a)
        d *= 2
    a_ref[...] = a
    x_ref[...] = x

    n_groups = tb // SUBLANES
    edge = 0 if reverse else SUBLANES - 1

    def body(gi, h):
        g = (n_groups - 1 - gi) if reverse else gi
        rows = pl.ds(pl.multiple_of(g * SUBLANES, SUBLANES), SUBLANES)
        hg = x_ref[rows, :] + a_ref[rows, :] * h
        if reverse:
            hg_out = (hg + hf_ref[0, rows, :]) * _gelu_tanh(gate_ref[0, rows, :])
        else:
            hg_out = hg
        o_ref[0, rows, :] = hg_out
        return hg[edge:edge + 1, :]

    h = lax.fori_loop(0, n_groups, body, h_ref[...], unroll=8)
    h_ref[...] = h

    @pl.when(j == pl.num_programs(1) - 1)
    def _():
        hout_ref[0] = h


def _rg_scan(p, conv_w, conv_b, wg, bg, ls, h0, reverse, row_conv, tb, hf=None):
    b, t, _ = p.shape
    nb = t // tb
    w = RG_WIDTH

    def blk(j):
        return (nb - 1 - j) if reverse else j

    def col(cidx):
        return pl.BlockSpec((1, tb, w), lambda i, j: (i, blk(j), cidx))

    h_spec = pl.BlockSpec((1, 1, w), lambda i, j: (i, 0, 0))
    in_specs = [col(COL_U), _const_spec((4, w)), _const_spec((1, w)),
                _const_spec((w, 2 * w)), _const_spec((1, 2 * w)), _const_spec((1, w)), h_spec]
    args = [p, conv_w, conv_b.reshape(1, w), wg, bg.reshape(1, 2 * w), ls.reshape(1, w), h0]
    if reverse:
        in_specs += [pl.BlockSpec((1, tb, w), lambda i, j: (i, blk(j), 0)), col(COL_GATE)]
        args += [hf, p]
    return pl.pallas_call(
        functools.partial(_rg_kernel, reverse=reverse, row_conv=row_conv, tb=tb),
        grid=(b, nb),
        in_specs=in_specs,
        out_specs=[pl.BlockSpec((1, tb, w), lambda i, j: (i, blk(j), 0)), h_spec],
        out_shape=[jax.ShapeDtypeStruct((b, t, w), F32), jax.ShapeDtypeStruct((b, 1, w), F32)],
        scratch_shapes=[pltpu.VMEM((tb, w), F32), pltpu.VMEM((tb, w), F32),
                        pltpu.VMEM((1, w), F32)],
        compiler_params=_params(("parallel", "arbitrary")),
        name="rglru_bwd" if reverse else "rglru_fwd",
    )(*args)


def _out_ffn_kernel(x_ref, hg_ref, rg_ref, m_ref, g_ref, wo_ref, wg_ref, wu_ref, wd_ref, o_ref):
    o = jnp.dot(hg_ref[0].astype(BF16), wo_ref[:HG_WIDTH, :], preferred_element_type=F32)
    o = o + jnp.dot(rg_ref[0].astype(BF16), wo_ref[HG_WIDTH:, :], preferred_element_type=F32)
    x1 = x_ref[0] + m_ref[0, 2:3, :] * _rms(o, g_ref[1:2, :])
    f = (_rms(x1, g_ref[2:3, :]) * (1.0 + m_ref[0, 4:5, :]) + m_ref[0, 3:4, :]).astype(BF16)
    gate = jnp.dot(f, wg_ref[...], preferred_element_type=F32)
    up = jnp.dot(f, wu_ref[...], preferred_element_type=F32)
    ff = jnp.dot((_silu(gate) * up).astype(BF16), wd_ref[...], preferred_element_type=F32)
    o_ref[0] = x1 + m_ref[0, 5:6, :] * _rms(ff, g_ref[3:4, :])


def _out_ffn(x, hg, rg, m, g, wo, wg, wu, wd, tm):
    b, t, d = x.shape
    mb = m.shape[0]
    dff = wg.shape[1]
    mixw = wo.shape[0]

    def resident(shape):
        return pl.BlockSpec(shape, lambda i, j: (0, 0), pipeline_mode=pl.Buffered(1))

    def tok(width):
        return pl.BlockSpec((1, tm, width), lambda i, j: (i, j, 0))

    return pl.pallas_call(
        _out_ffn_kernel,
        grid=(b, t // tm),
        in_specs=[tok(d), tok(HG_WIDTH), tok(RG_WIDTH),
                  pl.BlockSpec((1, N_MOD, d), lambda i, j: (i if mb > 1 else 0, 0, 0)),
                  _const_spec((4, d)),
                  resident((mixw, d)), resident((d, dff)), resident((d, dff)),
                  resident((dff, d))],
        out_specs=tok(d),
        out_shape=jax.ShapeDtypeStruct((b, t, d), F32),
        compiler_params=_params(("parallel", "parallel")),
        name="out_ffn",
    )(x, hg, rg, m, g, wo, wg, wu, wd)


def _block_diag(wb):
    nb, n, _ = wb.shape
    eye = jnp.eye(nb, dtype=wb.dtype)
    return jnp.einsum("nij,nm->nimj", wb, eye).reshape(nb * n, nb * n)


def _largest_tile(t, cap, step):
    best = step
    for cand in range(step, min(t, cap) + 1, step):
        if t % cand == 0:
            best = cand
    return best


def kernel(x, c, ctx, c_ctx, w_mod, b_mod, norm_g, w_in, hg_lb_logits, hg_gnorm, rg_conv_w, rg_conv_b, rg_w_a, rg_b_a, rg_w_x, rg_b_x, rg_lambda, w_out, w_ffn_gate, w_ffn_up, w_ffn_down):
    depth = w_mod.shape[0]
    bsz, seq, d = x.shape
    n_ctx = ctx.shape[1]

    lb_all = jnp.cumsum(jax.nn.softmax(hg_lb_logits.astype(F32), axis=0), axis=0)
    lb_all = lb_all - lb_all[0]
    log_sig = jax.nn.log_sigmoid(rg_lambda.astype(F32))

    rows = -(-(bsz + 1) // SUBLANES) * SUBLANES
    cvec = jnp.zeros((rows, d), F32).at[:bsz].set(c).at[bsz].set(c_ctx)
    mod = _modulation(cvec, w_mod, b_mod).reshape(depth, rows, N_MOD, d)

    tm_lat = _largest_tile(seq, 512, 128)
    tm_ctx = _largest_tile(n_ctx, 512, 128)
    tb_hg = _largest_tile(seq, 1024, CHUNK)
    tb_rg = _largest_tile(seq, 2048, GRID_W)

    x_lat, x_ctx = x, ctx
    for layer in range(depth):
        with_ctx = layer < depth - 1
        m_l = mod[layer, :bsz]
        m_c = mod[layer, bsz:bsz + 1]
        g = norm_g[layer]
        w_in_bf = w_in[layer].astype(BF16)
        p_l = _in_proj(x_lat, m_l, g[0], w_in_bf, tm_lat)
        p_c = _in_proj(x_ctx, m_c, g[0], w_in_bf, tm_ctx)

        s0 = jnp.zeros((bsz, HG_HEADS, HG_DK, HG_DK), F32)
        of_c, s_f = _hg_scan(p_c, lb_all[layer, 0], s0, False, n_ctx)
        of_l, _ = _hg_scan(p_l, lb_all[layer, 0], s_f, False, tb_hg)
        hg_c, s_b = _hg_scan(p_c, lb_all[layer, 1], s0, True, n_ctx, of=of_c,
                             gnorm=hg_gnorm[layer])
        hg_l, _ = _hg_scan(p_l, lb_all[layer, 1], s_b, True, tb_hg, of=of_l,
                           gnorm=hg_gnorm[layer])

        h0 = jnp.zeros((bsz, 1, RG_WIDTH), F32)
        cw = rg_conv_w[layer].astype(F32)
        cb = rg_conv_b[layer].astype(F32)
        wgs = [jnp.concatenate([_block_diag(rg_w_a[layer, dd]), _block_diag(rg_w_x[layer, dd])],
                               axis=1).astype(BF16) for dd in range(2)]
        bgs = [jnp.concatenate([rg_b_a[layer, dd], rg_b_x[layer, dd]]).astype(F32)
               for dd in range(2)]
        hf_c, h_f = _rg_scan(p_c, cw, cb, wgs[0], bgs[0], log_sig[layer, 0], h0, False, False,
                             n_ctx)
        hf_l, _ = _rg_scan(p_l, cw, cb, wgs[0], bgs[0], log_sig[layer, 0], h_f, False, True,
                           tb_rg)
        rg_c, h_b = _rg_scan(p_c, cw, cb, wgs[1], bgs[1], log_sig[layer, 1], h0, True, False,
                             n_ctx, hf=hf_c)
        rg_l, _ = _rg_scan(p_l, cw, cb, wgs[1], bgs[1], log_sig[layer, 1], h_b, True, True,
                           tb_rg, hf=hf_l)

        wo = w_out[layer].astype(BF16)
        wg_ = w_ffn_gate[layer].astype(BF16)
        wu_ = w_ffn_up[layer].astype(BF16)
        wd_ = w_ffn_down[layer].astype(BF16)
        x_lat = _out_ffn(x_lat, hg_l, rg_l, m_l, g, wo, wg_, wu_, wd_, tm_lat)
        if with_ctx:
            x_ctx = _out_ffn(x_ctx, hg_c, rg_c, m_c, g, wo, wg_, wu_, wd_, tm_ctx)
    return x_lat
```

```python
import functools

import numpy as np
import jax
import jax.numpy as jnp
from jax import lax
from jax.experimental import pallas as pl
from jax.experimental.pallas import tpu as pltpu

F32 = jnp.float32
BF16 = jnp.bfloat16

EPS = 1e-6
N_MOD = 6
HG_HEADS = 4
HG_DK = 128
HG_WIDTH = HG_HEADS * HG_DK
RG_WIDTH = 512
RG_BLOCKS = 8
RG_C = 8.0
GRID_W = 64
CHUNK = 128
LEVELS = (1, 2, 4, 8, 16, 32, 64)
SUBLANES = 8
LANES = 128
LOG2E = 1.4426950408889634
VMEM_LIMIT = 56 * 1024 * 1024

COL_Q, COL_ZF, COL_ZB, COL_V, COL_G, COL_U, COL_GATE = range(7)


def _params(sem):
    return pltpu.CompilerParams(dimension_semantics=sem, vmem_limit_bytes=VMEM_LIMIT)


def _const_spec(shape):
    nd = len(shape)
    return pl.BlockSpec(shape, lambda *_: (0,) * nd)


def _sigmoid(x):
    return 0.5 * jnp.tanh(0.5 * x) + 0.5


def _silu(x):
    h = 0.5 * x
    return h + h * jnp.tanh(h)


def _rms(x, g):
    return x * lax.rsqrt(jnp.mean(x * x, axis=-1, keepdims=True) + EPS) * g


def _mod_kernel(c_ref, w_ref, b_ref, o_ref):
    a = _silu(c_ref[...])
    o_ref[0] = jnp.dot(a, w_ref[0], precision=lax.Precision.HIGHEST,
                       preferred_element_type=F32) + b_ref[0]


def _modulation(cvec, w_mod, b_mod):
    depth, d, n = w_mod.shape
    rows = cvec.shape[0]
    nb = 4
    bn = n // nb
    return pl.pallas_call(
        _mod_kernel,
        grid=(depth, nb),
        in_specs=[
            pl.BlockSpec((rows, d), lambda l, j: (0, 0)),
            pl.BlockSpec((1, d, bn), lambda l, j: (l, 0, j)),
            pl.BlockSpec((1, 1, bn), lambda l, j: (l, 0, j)),
        ],
        out_specs=pl.BlockSpec((1, rows, bn), lambda l, j: (l, 0, j)),
        out_shape=jax.ShapeDtypeStruct((depth, rows, n), F32),
        compiler_params=_params(("parallel", "parallel")),
        name="adaln_mod",
    )(cvec, w_mod, b_mod.reshape(depth, 1, n))


def _proj_kernel(x_ref, m_ref, g_ref, w_ref, o_ref):
    h = _rms(x_ref[0], g_ref[...]) * (1.0 + m_ref[0, 1:2, :]) + m_ref[0, 0:1, :]
    o_ref[0] = jnp.dot(h.astype(BF16), w_ref[...], preferred_element_type=F32)


def _in_proj(x, m, g0, w_in_bf, tm):
    b, t, d = x.shape
    n = w_in_bf.shape[1]
    mb = m.shape[0]
    return pl.pallas_call(
        _proj_kernel,
        grid=(b, t // tm),
        in_specs=[
            pl.BlockSpec((1, tm, d), lambda i, j: (i, j, 0)),
            pl.BlockSpec((1, N_MOD, d), lambda i, j: (i if mb > 1 else 0, 0, 0)),
            _const_spec((1, d)),
            pl.BlockSpec((d, n), lambda i, j: (0, 0), pipeline_mode=pl.Buffered(1)),
        ],
        out_specs=pl.BlockSpec((1, tm, n), lambda i, j: (i, j, 0)),
        out_shape=jax.ShapeDtypeStruct((b, t, n), F32),
        compiler_params=_params(("parallel", "parallel")),
        name="in_proj",
    )(x, m, g0.reshape(1, d), w_in_bf)


def _level_masks(reverse):
    t = np.arange(CHUNK)[:, None]
    s = np.arange(CHUNK)[None, :]
    if reverse:
        t, s = s, t
    masks = [t == s]
    for b in LEVELS:
        masks.append((t // b == s // b + 1) & ((s // b) % 2 == 0))
    return np.stack(masks).astype(np.float32)


def _hg_chunk(q_raw, z, v, lb, st_ref, mask_ref, reverse):
    n = CHUNK
    w = q_raw.shape[1]
    ng = n // SUBLANES
    q = _silu(q_raw)
    e = jnp.exp(-jnp.abs(z))
    r = 1.0 / (1.0 + e)
    pos = z >= 0
    sig = jnp.where(pos, r, e * r)
    sig_neg = jnp.where(pos, e * r, r)
    k = (1.0 - lb) * sig_neg
    f = lb + (1.0 - lb) * sig
    lf2 = jnp.log(f) * LOG2E

    sub = lax.broadcasted_iota(jnp.int32, (ng, SUBLANES, w), 1)

    def later(b):
        on = (sub & b) != 0
        return jnp.logical_not(on) if reverse else on

    g3 = lf2.reshape(ng, SUBLANES, w)
    pos_in_group = (SUBLANES - 1 - sub) if reverse else sub
    d = 1
    while d < SUBLANES:
        sh = pltpu.roll(g3, (SUBLANES - d) if reverse else d, 1)
        g3 = g3 + jnp.where(pos_in_group >= d, sh, 0.0)
        d *= 2
    edge = 0 if reverse else SUBLANES - 1
    groups = [None] * ng
    carry = None
    for g in (range(ng - 1, -1, -1) if reverse else range(ng)):
        blk = g3[g] if carry is None else g3[g] + carry
        groups[g] = blk
        carry = blk[edge:edge + 1, :]
    lc = jnp.concatenate(groups, axis=0)
    lc3 = lc.reshape(ng, SUBLANES, w)
    q3 = q.reshape(ng, SUBLANES, w)
    k3 = k.reshape(ng, SUBLANES, w)

    acc = [None] * HG_HEADS
    vb = v.astype(BF16)

    def add_scores(lvl, a_bf, b_bf):
        for h in range(HG_HEADS):
            sl = slice(h * HG_DK, (h + 1) * HG_DK)
            sc = lax.dot_general(a_bf[:, sl], b_bf[:, sl], (((1,), (1,)), ((), ())),
                                 preferred_element_type=F32) * mask_ref[lvl]
            acc[h] = sc if acc[h] is None else acc[h] + sc

    def bcast_row(r):
        return jnp.broadcast_to(lc3[:, r:r + 1, :], (ng, SUBLANES, w))

    add_scores(0, q.astype(BF16), k.astype(BF16))
    for lvl, b in enumerate(LEVELS):
        if b == 1:
            xb = jnp.where(later(1), q3 * f.reshape(ng, SUBLANES, w), k3)
        elif b < SUBLANES:
            rows = [sb * 2 * b + (b if reverse else b - 1) for sb in range(SUBLANES // (2 * b))]
            anchor = bcast_row(rows[0])
            for i in range(1, len(rows)):
                anchor = jnp.where(sub >= i * 2 * b, bcast_row(rows[i]), anchor)
            wgt = jnp.exp2(-jnp.abs(lc3 - anchor))
            xb = jnp.where(later(b), q3, k3) * wgt
        else:
            pieces = []
            for sb in range(n // (2 * b)):
                lo, mid, hi = sb * 2 * b, sb * 2 * b + b, (sb + 1) * 2 * b
                if reverse:
                    anc = lc[mid:mid + 1, :]
                    pieces.append(q[lo:mid] * jnp.exp2(lc[lo:mid] - anc))
                    pieces.append(k[mid:hi] * jnp.exp2(anc - lc[mid:hi]))
                else:
                    anc = lc[mid - 1:mid, :]
                    pieces.append(k[lo:mid] * jnp.exp2(anc - lc[lo:mid]))
                    pieces.append(q[mid:hi] * jnp.exp2(lc[mid:hi] - anc))
            xb = jnp.concatenate(pieces, axis=0)
        xb = xb.reshape(n, w).astype(BF16)
        add_scores(lvl + 1, xb, xb)

    last = lc[0:1, :] if reverse else lc[n - 1:n, :]
    q_in = (q * jnp.exp2(lc)).astype(BF16)
    k_out = (k * jnp.exp2(last - lc)).astype(BF16)
    total = jnp.exp2(last)
    outs = []
    for h in range(HG_HEADS):
        sl = slice(h * HG_DK, (h + 1) * HG_DK)
        st = st_ref[h]
        o = jnp.dot(acc[h].astype(BF16), vb[:, sl], preferred_element_type=F32)
        o = o + lax.dot_general(q_in[:, sl], st.astype(BF16), (((1,), (1,)), ((), ())),
                                preferred_element_type=F32)
        st_ref[h] = st * total[:, sl] + lax.dot_general(
            vb[:, sl], k_out[:, sl], (((0,), (0,)), ((), ())), preferred_element_type=F32)
        outs.append(o)
    return jnp.concatenate(outs, axis=1)


def _hg_kernel(*refs, reverse, readout, n_chunks):
    if readout:
        (q_ref, z_ref, v_ref, lb_ref, s0_ref, mask_ref, of_ref, g_ref, gn_ref,
         o_ref, sout_ref, st_ref) = refs
    else:
        q_ref, z_ref, v_ref, lb_ref, s0_ref, mask_ref, o_ref, sout_ref, st_ref = refs
    j = pl.program_id(1)

    @pl.when(j == 0)
    def _():
        st_ref[...] = s0_ref[0]

    lb = lb_ref[...]

    def body(ci, carry):
        c = (n_chunks - 1 - ci) if reverse else ci
        rows = pl.ds(pl.multiple_of(c * CHUNK, CHUNK), CHUNK)
        o = _hg_chunk(q_ref[0, rows, :], z_ref[0, rows, :], v_ref[0, rows, :], lb,
                      st_ref, mask_ref, reverse)
        if readout:
            o = o + of_ref[0, rows, :]
            g = g_ref[0, rows, :]
            parts = []
            for h in range(HG_HEADS):
                sl = slice(h * HG_DK, (h + 1) * HG_DK)
                parts.append(_rms(o[:, sl], gn_ref[...]))
            o = jnp.concatenate(parts, axis=1) * _silu(g)
        o_ref[0, rows, :] = o
        return carry

    lax.fori_loop(0, n_chunks, body, 0, unroll=2)

    @pl.when(j == pl.num_programs(1) - 1)
    def _():
        sout_ref[0] = st_ref[...]


def _hg_scan(p, lb, s0, reverse, tb, of=None, gnorm=None):
    b, t, _ = p.shape
    nb = t // tb
    readout = of is not None
    masks = jnp.asarray(_level_masks(reverse))

    def blk(j):
        return (nb - 1 - j) if reverse else j

    def col(cidx):
        return pl.BlockSpec((1, tb, HG_WIDTH), lambda i, j: (i, blk(j), cidx))

    state_spec = pl.BlockSpec((1, HG_HEADS, HG_DK, HG_DK), lambda i, j: (i, 0, 0, 0))
    in_specs = [col(COL_Q), col(COL_ZB if reverse else COL_ZF), col(COL_V),
                _const_spec((1, HG_WIDTH)), state_spec,
                _const_spec((len(LEVELS) + 1, CHUNK, CHUNK))]
    args = [p, p, p, lb.reshape(1, HG_WIDTH), s0, masks]
    if readout:
        in_specs += [pl.BlockSpec((1, tb, HG_WIDTH), lambda i, j: (i, blk(j), 0)),
                     col(COL_G), _const_spec((1, HG_DK))]
        args += [of, p, gnorm.reshape(1, HG_DK)]
    return pl.pallas_call(
        functools.partial(_hg_kernel, reverse=reverse, readout=readout,
                          n_chunks=tb // CHUNK),
        grid=(b, nb),
        in_specs=in_specs,
        out_specs=[pl.BlockSpec((1, tb, HG_WIDTH), lambda i, j: (i, blk(j), 0)), state_spec],
        out_shape=[jax.ShapeDtypeStruct((b, t, HG_WIDTH), F32),
                   jax.ShapeDtypeStruct((b, HG_HEADS, HG_DK, HG_DK), F32)],
        scratch_shapes=[pltpu.VMEM((HG_HEADS, HG_DK, HG_DK), F32)],
        compiler_params=_params(("parallel", "arbitrary")),
        name="hgrn2_bwd" if reverse else "hgrn2_fwd",
    )(*args)


def _gelu_tanh(x):
    return 0.5 * x * (1.0 + jnp.tanh(0.7978845608028654 * (x + 0.044715 * (x * x * x))))


def _rg_kernel(*refs, reverse, row_conv, tb):
    if reverse:
        (u_ref, cw_ref, cb_ref, wg_ref, bg_ref, ls_ref, h0_ref, hf_ref, gate_ref,
         o_ref, hout_ref, a_scr, x_scr, hs_scr, hl_scr, ac_scr, h_ref, up_scr) = refs
    else:
        (u_ref, cw_ref, cb_ref, wg_ref, bg_ref, ls_ref, h0_ref,
         o_ref, hout_ref, a_scr, x_scr, hs_scr, hl_scr, ac_scr, h_ref, up_scr) = refs
    j = pl.program_id(1)

    @pl.when(j == 0)
    def _():
        h_ref[...] = h0_ref[0]

    u = u_ref[0]
    w = u.shape[1]
    row = lax.broadcasted_iota(jnp.int32, (tb, w), 0)
    pos = (row & (GRID_W - 1)) if row_conv else row
    last = (GRID_W if row_conv else tb) - 1
    up_scr[0:SUBLANES, :] = jnp.zeros((SUBLANES, w), F32)
    up_scr[SUBLANES + tb:, :] = jnp.zeros((SUBLANES, w), F32)
    up_scr[SUBLANES:SUBLANES + tb, :] = u
    uc = cw_ref[2:3, :] * u + cb_ref[...]
    uc = uc + cw_ref[0:1, :] * jnp.where(pos >= 2, up_scr[pl.ds(SUBLANES - 2, tb), :], 0.0)
    uc = uc + cw_ref[1:2, :] * jnp.where(pos >= 1, up_scr[pl.ds(SUBLANES - 1, tb), :], 0.0)
    uc = uc + cw_ref[3:4, :] * jnp.where(pos < last, up_scr[pl.ds(SUBLANES + 1, tb), :], 0.0)

    gates = jnp.dot(uc.astype(BF16), wg_ref[...], preferred_element_type=F32) + bg_ref[...]
    r = _sigmoid(gates[:, :w])
    i = _sigmoid(gates[:, w:])
    a = jnp.exp2(r * ((RG_C * LOG2E) * ls_ref[...]))
    y = 1.0 - a * a
    x = jnp.where(y > 0.0, y * lax.rsqrt(y), 0.0) * (i * uc)

    seg = tb // SUBLANES
    pitch = seg + SUBLANES
    nsl = w // LANES
    for c in range(nsl):
        for sgm in range(SUBLANES):
            a_scr[c, sgm * pitch:sgm * pitch + seg, :] = a[sgm * seg:(sgm + 1) * seg,
                                                           c * LANES:(c + 1) * LANES]
            x_scr[c, sgm * pitch:sgm * pitch + seg, :] = x[sgm * seg:(sgm + 1) * seg,
                                                           c * LANES:(c + 1) * LANES]

    def pass1(ii, carry):
        i = (seg - 1 - ii) if reverse else ii
        hs, cums = carry
        picked = pl.ds(i, SUBLANES, stride=pitch)
        rows = pl.ds(pl.multiple_of(i * SUBLANES, SUBLANES), SUBLANES)
        new_h, new_c = [], []
        for c in range(nsl):
            ai = a_scr[c, picked, :]
            hc = ai * hs[c] + x_scr[c, picked, :]
            cc = ai * cums[c]
            hl_scr[c, rows, :] = hc
            ac_scr[c, rows, :] = cc
            new_h.append(hc)
            new_c.append(cc)
        return tuple(new_h), tuple(new_c)

    zeros = tuple(jnp.zeros((SUBLANES, LANES), F32) for _ in range(nsl))
    ones = tuple(jnp.ones((SUBLANES, LANES), F32) for _ in range(nsl))
    h_end, a_tot = lax.fori_loop(0, seg, pass1, (zeros, ones), unroll=4)

    h_in = h_ref[...]
    enter, leave = [], []
    for c in range(nsl):
        cur = h_in[:, c * LANES:(c + 1) * LANES]
        per_seg = [None] * SUBLANES
        for sgm in (range(SUBLANES - 1, -1, -1) if reverse else range(SUBLANES)):
            per_seg[sgm] = cur
            cur = h_end[c][sgm:sgm + 1, :] + a_tot[c][sgm:sgm + 1, :] * cur
        enter.append(jnp.concatenate(per_seg, axis=0))
        leave.append(cur)
    h_out = jnp.concatenate(leave, axis=1)
    h_ref[...] = h_out

    def pass2(i, carry):
        picked = pl.ds(i, SUBLANES, stride=pitch)
        rows = pl.ds(pl.multiple_of(i * SUBLANES, SUBLANES), SUBLANES)
        for c in range(nsl):
            hs_scr[c, picked, :] = hl_scr[c, rows, :] + ac_scr[c, rows, :] * enter[c]
        return carry

    lax.fori_loop(0, seg, pass2, 0, unroll=4)

    for sgm in range(SUBLANES):
        rows = slice(sgm * seg, (sgm + 1) * seg)
        hn = jnp.concatenate([hs_scr[c, sgm * pitch:sgm * pitch + seg, :] for c in range(nsl)],
                             axis=1)
        if reverse:
            hn = (hn + hf_ref[0, rows, :]) * _gelu_tanh(gate_ref[0, rows, :])
        o_ref[0, rows, :] = hn

    @pl.when(j == pl.num_programs(1) - 1)
    def _():
        hout_ref[0] = h_out


def _rg_scan(p, conv_w, conv_b, wg, bg, ls, h0, reverse, row_conv, tb, hf=None):
    b, t, _ = p.shape
    nb = t // tb
    w = RG_WIDTH

    def blk(j):
        return (nb - 1 - j) if reverse else j

    def col(cidx):
        return pl.BlockSpec((1, tb, w), lambda i, j: (i, blk(j), cidx))

    h_spec = pl.BlockSpec((1, 1, w), lambda i, j: (i, 0, 0))
    in_specs = [col(COL_U), _const_spec((4, w)), _const_spec((1, w)),
                _const_spec((w, 2 * w)), _const_spec((1, 2 * w)), _const_spec((1, w)), h_spec]
    args = [p, conv_w, conv_b.reshape(1, w), wg, bg.reshape(1, 2 * w), ls.reshape(1, w), h0]
    if reverse:
        in_specs += [pl.BlockSpec((1, tb, w), lambda i, j: (i, blk(j), 0)), col(COL_GATE)]
        args += [hf, p]
    return pl.pallas_call(
        functools.partial(_rg_kernel, reverse=reverse, row_conv=row_conv, tb=tb),
        grid=(b, nb),
        in_specs=in_specs,
        out_specs=[pl.BlockSpec((1, tb, w), lambda i, j: (i, blk(j), 0)), h_spec],
        out_shape=[jax.ShapeDtypeStruct((b, t, w), F32), jax.ShapeDtypeStruct((b, 1, w), F32)],
        scratch_shapes=[pltpu.VMEM((w // LANES, tb + SUBLANES * SUBLANES, LANES), F32)] * 3
        + [pltpu.VMEM((w // LANES, tb, LANES), F32)] * 2 + [pltpu.VMEM((1, w), F32)]
        + [pltpu.VMEM((tb + 2 * SUBLANES, w), F32)],
        compiler_params=_params(("parallel", "arbitrary")),
        name="rglru_bwd" if reverse else "rglru_fwd",
    )(*args)


def _out_ffn_kernel(x_ref, hg_ref, rg_ref, m_ref, g_ref, wo_ref, wg_ref, wu_ref, wd_ref, o_ref):
    o = jnp.dot(hg_ref[0].astype(BF16), wo_ref[:HG_WIDTH, :], preferred_element_type=F32)
    o = o + jnp.dot(rg_ref[0].astype(BF16), wo_ref[HG_WIDTH:, :], preferred_element_type=F32)
    x1 = x_ref[0] + m_ref[0, 2:3, :] * _rms(o, g_ref[1:2, :])
    f = (_rms(x1, g_ref[2:3, :]) * (1.0 + m_ref[0, 4:5, :]) + m_ref[0, 3:4, :]).astype(BF16)
    gate = jnp.dot(f, wg_ref[...], preferred_element_type=F32)
    up = jnp.dot(f, wu_ref[...], preferred_element_type=F32)
    ff = jnp.dot((_silu(gate) * up).astype(BF16), wd_ref[...], preferred_element_type=F32)
    o_ref[0] = x1 + m_ref[0, 5:6, :] * _rms(ff, g_ref[3:4, :])


def _out_ffn(x, hg, rg, m, g, wo, wg, wu, wd, tm):
    b, t, d = x.shape
    mb = m.shape[0]
    dff = wg.shape[1]
    mixw = wo.shape[0]

    def resident(shape):
        return pl.BlockSpec(shape, lambda i, j: (0, 0), pipeline_mode=pl.Buffered(1))

    def tok(width):
        return pl.BlockSpec((1, tm, width), lambda i, j: (i, j, 0))

    return pl.pallas_call(
        _out_ffn_kernel,
        grid=(b, t // tm),
        in_specs=[tok(d), tok(HG_WIDTH), tok(RG_WIDTH),
                  pl.BlockSpec((1, N_MOD, d), lambda i, j: (i if mb > 1 else 0, 0, 0)),
                  _const_spec((4, d)),
                  resident((mixw, d)), resident((d, dff)), resident((d, dff)),
                  resident((dff, d))],
        out_specs=tok(d),
        out_shape=jax.ShapeDtypeStruct((b, t, d), F32),
        compiler_params=_params(("parallel", "parallel")),
        name="out_ffn",
    )(x, hg, rg, m, g, wo, wg, wu, wd)


def _block_diag(wb):
    nb, n, _ = wb.shape
    eye = jnp.eye(nb, dtype=wb.dtype)
    return jnp.einsum("nij,nm->nimj", wb, eye).reshape(nb * n, nb * n)


def _largest_tile(t, cap, step):
    best = step
    for cand in range(step, min(t, cap) + 1, step):
        if t % cand == 0:
            best = cand
    return best


def kernel(x, c, ctx, c_ctx, w_mod, b_mod, norm_g, w_in, hg_lb_logits, hg_gnorm, rg_conv_w, rg_conv_b, rg_w_a, rg_b_a, rg_w_x, rg_b_x, rg_lambda, w_out, w_ffn_gate, w_ffn_up, w_ffn_down):
    depth = w_mod.shape[0]
    bsz, seq, d = x.shape
    n_ctx = ctx.shape[1]

    lb_all = jnp.cumsum(jax.nn.softmax(hg_lb_logits.astype(F32), axis=0), axis=0)
    lb_all = lb_all - lb_all[0]
    log_sig = jax.nn.log_sigmoid(rg_lambda.astype(F32))

    rows = -(-(bsz + 1) // SUBLANES) * SUBLANES
    cvec = jnp.zeros((rows, d), F32).at[:bsz].set(c).at[bsz].set(c_ctx)
    mod = _modulation(cvec, w_mod, b_mod).reshape(depth, rows, N_MOD, d)

    tm_lat = _largest_tile(seq, 512, 128)
    tm_ctx = _largest_tile(n_ctx, 512, 128)
    tb_hg = _largest_tile(seq, 1024, CHUNK)
    tb_rg = _largest_tile(seq, 1024, 2 * GRID_W)

    x_lat, x_ctx = x, ctx
    for layer in range(depth):
        with_ctx = layer < depth - 1
        m_l = mod[layer, :bsz]
        m_c = mod[layer, bsz:bsz + 1]
        g = norm_g[layer]
        w_in_bf = w_in[layer].astype(BF16)
        p_l = _in_proj(x_lat, m_l, g[0], w_in_bf, tm_lat)
        p_c = _in_proj(x_ctx, m_c, g[0], w_in_bf, tm_ctx)

        s0 = jnp.zeros((bsz, HG_HEADS, HG_DK, HG_DK), F32)
        of_c, s_f = _hg_scan(p_c, lb_all[layer, 0], s0, False, n_ctx)
        of_l, _ = _hg_scan(p_l, lb_all[layer, 0], s_f, False, tb_hg)
        hg_c, s_b = _hg_scan(p_c, lb_all[layer, 1], s0, True, n_ctx, of=of_c,
                             gnorm=hg_gnorm[layer])
        hg_l, _ = _hg_scan(p_l, lb_all[layer, 1], s_b, True, tb_hg, of=of_l,
                           gnorm=hg_gnorm[layer])

        h0 = jnp.zeros((bsz, 1, RG_WIDTH), F32)
        cw = rg_conv_w[layer].astype(F32)
        cb = rg_conv_b[layer].astype(F32)
        wgs = [jnp.concatenate([_block_diag(rg_w_a[layer, dd]), _block_diag(rg_w_x[layer, dd])],
                               axis=1).astype(BF16) for dd in range(2)]
        bgs = [jnp.concatenate([rg_b_a[layer, dd], rg_b_x[layer, dd]]).astype(F32)
               for dd in range(2)]
        hf_c, h_f = _rg_scan(p_c, cw, cb, wgs[0], bgs[0], log_sig[layer, 0], h0, False, False,
                             n_ctx)
        hf_l, _ = _rg_scan(p_l, cw, cb, wgs[0], bgs[0], log_sig[layer, 0], h_f, False, True,
                           tb_rg)
        rg_c, h_b = _rg_scan(p_c, cw, cb, wgs[1], bgs[1], log_sig[layer, 1], h0, True, False,
                             n_ctx, hf=hf_c)
        rg_l, _ = _rg_scan(p_l, cw, cb, wgs[1], bgs[1], log_sig[layer, 1], h_b, True, True,
                           tb_rg, hf=hf_l)

        wo = w_out[layer].astype(BF16)
        wg_ = w_ffn_gate[layer].astype(BF16)
        wu_ = w_ffn_up[layer].astype(BF16)
        wd_ = w_ffn_down[layer].astype(BF16)
        x_lat = _out_ffn(x_lat, hg_l, rg_l, m_l, g, wo, wg_, wu_, wd_, tm_lat)
        if with_ctx:
            x_ctx = _out_ffn(x_ctx, hg_c, rg_c, m_c, g, wo, wg_, wu_, wd_, tm_ctx)
    return x_lat
```

```python
import functools

import numpy as np
import jax
import jax.numpy as jnp
from jax import lax
from jax.experimental import pallas as pl
from jax.experimental.pallas import tpu as pltpu

F32 = jnp.float32
BF16 = jnp.bfloat16

EPS = 1e-6
N_MOD = 6
HG_HEADS = 4
HG_DK = 128
HG_WIDTH = HG_HEADS * HG_DK
RG_WIDTH = 512
RG_BLOCKS = 8
RG_C = 8.0
GRID_W = 64
CHUNK = 128
LEVELS = (1, 2, 4, 8, 16, 32, 64)
SUBLANES = 8
LANES = 128
RG_SEGS = 16
LOG2E = 1.4426950408889634
VMEM_LIMIT = 56 * 1024 * 1024

COL_Q, COL_ZF, COL_ZB, COL_V, COL_G, COL_U, COL_GATE = range(7)


def _params(sem):
    return pltpu.CompilerParams(dimension_semantics=sem, vmem_limit_bytes=VMEM_LIMIT)


def _const_spec(shape):
    nd = len(shape)
    return pl.BlockSpec(shape, lambda *_: (0,) * nd)


def _sigmoid(x):
    return 0.5 * jnp.tanh(0.5 * x) + 0.5


def _silu(x):
    h = 0.5 * x
    return h + h * jnp.tanh(h)


def _rms(x, g):
    return x * lax.rsqrt(jnp.mean(x * x, axis=-1, keepdims=True) + EPS) * g


def _mod_kernel(c_ref, w_ref, b_ref, o_ref):
    a = _silu(c_ref[...])
    o_ref[0] = jnp.dot(a, w_ref[0], precision=lax.Precision.HIGHEST,
                       preferred_element_type=F32) + b_ref[0]


def _modulation(cvec, w_mod, b_mod):
    depth, d, n = w_mod.shape
    rows = cvec.shape[0]
    nb = 4
    bn = n // nb
    return pl.pallas_call(
        _mod_kernel,
        grid=(depth, nb),
        in_specs=[
            pl.BlockSpec((rows, d), lambda l, j: (0, 0)),
            pl.BlockSpec((1, d, bn), lambda l, j: (l, 0, j)),
            pl.BlockSpec((1, 1, bn), lambda l, j: (l, 0, j)),
        ],
        out_specs=pl.BlockSpec((1, rows, bn), lambda l, j: (l, 0, j)),
        out_shape=jax.ShapeDtypeStruct((depth, rows, n), F32),
        compiler_params=_params(("parallel", "parallel")),
        name="adaln_mod",
    )(cvec, w_mod, b_mod.reshape(depth, 1, n))


def _proj_kernel(x_ref, m_ref, g_ref, w_ref, o_ref):
    h = _rms(x_ref[0], g_ref[...]) * (1.0 + m_ref[0, 1:2, :]) + m_ref[0, 0:1, :]
    o_ref[0] = jnp.dot(h.astype(BF16), w_ref[...], preferred_element_type=F32)


def _in_proj(x, m, g0, w_in_bf, tm):
    b, t, d = x.shape
    n = w_in_bf.shape[1]
    mb = m.shape[0]
    return pl.pallas_call(
        _proj_kernel,
        grid=(b, t // tm),
        in_specs=[
            pl.BlockSpec((1, tm, d), lambda i, j: (i, j, 0)),
            pl.BlockSpec((1, N_MOD, d), lambda i, j: (i if mb > 1 else 0, 0, 0)),
            _const_spec((1, d)),
            pl.BlockSpec((d, n), lambda i, j: (0, 0), pipeline_mode=pl.Buffered(1)),
        ],
        out_specs=pl.BlockSpec((1, tm, n), lambda i, j: (i, j, 0)),
        out_shape=jax.ShapeDtypeStruct((b, t, n), F32),
        compiler_params=_params(("parallel", "parallel")),
        name="in_proj",
    )(x, m, g0.reshape(1, d), w_in_bf)


def _level_masks(reverse):
    t = np.arange(CHUNK)[:, None]
    s = np.arange(CHUNK)[None, :]
    if reverse:
        t, s = s, t
    masks = [t == s]
    for b in LEVELS:
        masks.append((t // b == s // b + 1) & ((s // b) % 2 == 0))
    return np.stack(masks).astype(np.float32)


def _hg_chunk(q_raw, z, v, lb, st_ref, mask_ref, reverse):
    n = CHUNK
    w = q_raw.shape[1]
    ng = n // SUBLANES
    q = _silu(q_raw)
    e = jnp.exp(-jnp.abs(z))
    r = 1.0 / (1.0 + e)
    pos = z >= 0
    sig = jnp.where(pos, r, e * r)
    sig_neg = jnp.where(pos, e * r, r)
    k = (1.0 - lb) * sig_neg
    f = lb + (1.0 - lb) * sig
    lf2 = jnp.log(f) * LOG2E

    sub = lax.broadcasted_iota(jnp.int32, (ng, SUBLANES, w), 1)

    def later(b):
        on = (sub & b) != 0
        return jnp.logical_not(on) if reverse else on

    g3 = lf2.reshape(ng, SUBLANES, w)
    pos_in_group = (SUBLANES - 1 - sub) if reverse else sub
    d = 1
    while d < SUBLANES:
        sh = pltpu.roll(g3, (SUBLANES - d) if reverse else d, 1)
        g3 = g3 + jnp.where(pos_in_group >= d, sh, 0.0)
        d *= 2
    edge = 0 if reverse else SUBLANES - 1
    groups = [None] * ng
    carry = None
    for g in (range(ng - 1, -1, -1) if reverse else range(ng)):
        blk = g3[g] if carry is None else g3[g] + carry
        groups[g] = blk
        carry = blk[edge:edge + 1, :]
    lc = jnp.concatenate(groups, axis=0)
    lc3 = lc.reshape(ng, SUBLANES, w)
    q3 = q.reshape(ng, SUBLANES, w)
    k3 = k.reshape(ng, SUBLANES, w)

    acc = [None] * HG_HEADS
    vb = v.astype(BF16)

    def add_scores(lvl, a_bf, b_bf):
        for h in range(HG_HEADS):
            sl = slice(h * HG_DK, (h + 1) * HG_DK)
            sc = lax.dot_general(a_bf[:, sl], b_bf[:, sl], (((1,), (1,)), ((), ())),
                                 preferred_element_type=F32).astype(BF16) * mask_ref[lvl]
            acc[h] = sc if acc[h] is None else acc[h] + sc

    def bcast_row(r):
        return jnp.broadcast_to(lc3[:, r:r + 1, :], (ng, SUBLANES, w))

    add_scores(0, q.astype(BF16), k.astype(BF16))
    for lvl, b in enumerate(LEVELS):
        if b == 1:
            xb = jnp.where(later(1), q3 * f.reshape(ng, SUBLANES, w), k3)
        elif b < SUBLANES:
            rows = [sb * 2 * b + (b if reverse else b - 1) for sb in range(SUBLANES // (2 * b))]
            anchor = bcast_row(rows[0])
            for i in range(1, len(rows)):
                anchor = jnp.where(sub >= i * 2 * b, bcast_row(rows[i]), anchor)
            wgt = jnp.exp2(-jnp.abs(lc3 - anchor))
            xb = jnp.where(later(b), q3, k3) * wgt
        else:
            pieces = []
            for sb in range(n // (2 * b)):
                lo, mid, hi = sb * 2 * b, sb * 2 * b + b, (sb + 1) * 2 * b
                if reverse:
                    anc = lc[mid:mid + 1, :]
                    pieces.append(q[lo:mid] * jnp.exp2(lc[lo:mid] - anc))
                    pieces.append(k[mid:hi] * jnp.exp2(anc - lc[mid:hi]))
                else:
                    anc = lc[mid - 1:mid, :]
                    pieces.append(k[lo:mid] * jnp.exp2(anc - lc[lo:mid]))
                    pieces.append(q[mid:hi] * jnp.exp2(lc[mid:hi] - anc))
            xb = jnp.concatenate(pieces, axis=0)
        xb = xb.reshape(n, w).astype(BF16)
        add_scores(lvl + 1, xb, xb)

    last = lc[0:1, :] if reverse else lc[n - 1:n, :]
    q_in = (q * jnp.exp2(lc)).astype(BF16)
    k_out = (k * jnp.exp2(last - lc)).astype(BF16)
    total = jnp.exp2(last)
    outs = []
    for h in range(HG_HEADS):
        sl = slice(h * HG_DK, (h + 1) * HG_DK)
        st = st_ref[h]
        o = jnp.dot(acc[h], vb[:, sl], preferred_element_type=F32)
        o = o + lax.dot_general(q_in[:, sl], st.astype(BF16), (((1,), (1,)), ((), ())),
                                preferred_element_type=F32)
        st_ref[h] = st * total[:, sl] + lax.dot_general(
            vb[:, sl], k_out[:, sl], (((0,), (0,)), ((), ())), preferred_element_type=F32)
        outs.append(o)
    return jnp.concatenate(outs, axis=1)


def _hg_kernel(*refs, reverse, readout, n_chunks):
    if readout:
        (q_ref, z_ref, v_ref, lb_ref, s0_ref, mask_ref, of_ref,
         o_ref, sout_ref, st_ref) = refs
    else:
        q_ref, z_ref, v_ref, lb_ref, s0_ref, mask_ref, o_ref, sout_ref, st_ref = refs
    j = pl.program_id(1)

    @pl.when(j == 0)
    def _():
        st_ref[...] = s0_ref[0]

    lb = lb_ref[...]

    def body(ci, carry):
        c = (n_chunks - 1 - ci) if reverse else ci
        rows = pl.ds(pl.multiple_of(c * CHUNK, CHUNK), CHUNK)
        o = _hg_chunk(q_ref[0, rows, :], z_ref[0, rows, :], v_ref[0, rows, :], lb,
                      st_ref, mask_ref, reverse)
        if readout:
            o = o + of_ref[0, rows, :]
        o_ref[0, rows, :] = o
        return carry

    lax.fori_loop(0, n_chunks, body, 0, unroll=2)

    @pl.when(j == pl.num_programs(1) - 1)
    def _():
        sout_ref[0] = st_ref[...]


def _hg_scan(p, lb, s0, reverse, tb, of=None):
    b, t, _ = p.shape
    nb = t // tb
    readout = of is not None
    masks = jnp.asarray(_level_masks(reverse), dtype=BF16)

    def blk(j):
        return (nb - 1 - j) if reverse else j

    def col(cidx):
        return pl.BlockSpec((1, tb, HG_WIDTH), lambda i, j: (i, blk(j), cidx))

    state_spec = pl.BlockSpec((1, HG_HEADS, HG_DK, HG_DK), lambda i, j: (i, 0, 0, 0))
    in_specs = [col(COL_Q), col(COL_ZB if reverse else COL_ZF), col(COL_V),
                _const_spec((1, HG_WIDTH)), state_spec,
                _const_spec((len(LEVELS) + 1, CHUNK, CHUNK))]
    args = [p, p, p, lb.reshape(1, HG_WIDTH), s0, masks]
    if readout:
        in_specs += [pl.BlockSpec((1, tb, HG_WIDTH), lambda i, j: (i, blk(j), 0))]
        args += [of]
    return pl.pallas_call(
        functools.partial(_hg_kernel, reverse=reverse, readout=readout,
                          n_chunks=tb // CHUNK),
        grid=(b, nb),
        in_specs=in_specs,
        out_specs=[pl.BlockSpec((1, tb, HG_WIDTH), lambda i, j: (i, blk(j), 0)), state_spec],
        out_shape=[jax.ShapeDtypeStruct((b, t, HG_WIDTH), F32),
                   jax.ShapeDtypeStruct((b, HG_HEADS, HG_DK, HG_DK), F32)],
        scratch_shapes=[pltpu.VMEM((HG_HEADS, HG_DK, HG_DK), F32)],
        compiler_params=_params(("parallel", "arbitrary")),
        name="hgrn2_bwd" if reverse else "hgrn2_fwd",
    )(*args)


def _gelu_tanh(x):
    return 0.5 * x * (1.0 + jnp.tanh(0.7978845608028654 * (x + 0.044715 * (x * x * x))))


def _rg_kernel(*refs, reverse, row_conv, tb):
    (u_ref, cw_ref, cb_ref, wg_ref, bg_ref, ls_ref, h0_ref,
     o_ref, hout_ref, a_scr, x_scr, hs_scr, hl_scr, ac_scr, h_ref, up_scr) = refs
    j = pl.program_id(1)

    @pl.when(j == 0)
    def _():
        h_ref[...] = h0_ref[0]

    u = u_ref[0]
    w = u.shape[1]
    seg = tb // RG_SEGS
    pitch = seg + SUBLANES
    nsl = w // LANES

    def interleave(val, stage_scr, dst_scr):
        for c in range(nsl):
            for sgm in range(RG_SEGS):
                stage_scr[c, sgm * pitch:sgm * pitch + seg, :] = val[sgm * seg:(sgm + 1) * seg,
                                                                     c * LANES:(c + 1) * LANES]

        def body(i, carry):
            rows = pl.ds(pl.multiple_of(i * RG_SEGS, RG_SEGS), RG_SEGS)
            for c in range(nsl):
                dst_scr[rows, c * LANES:(c + 1) * LANES] = stage_scr[
                    c, pl.ds(i, RG_SEGS, stride=pitch), :]
            return carry

        lax.fori_loop(0, seg, body, 0, unroll=8)

    def coeffs(uc):
        gates = jnp.dot(uc.astype(BF16), wg_ref[...], preferred_element_type=F32) + bg_ref[...]
        r = _sigmoid(gates[:, :w])
        i = _sigmoid(gates[:, w:])
        a = jnp.exp2(r * ((RG_C * LOG2E) * ls_ref[...]))
        y = 1.0 - a * a
        return a, jnp.where(y > 0.0, y * lax.rsqrt(y), 0.0) * (i * uc)

    if row_conv:
        interleave(u, a_scr, up_scr)
        n_rows = seg // GRID_W
        u4 = up_scr[0:tb, :].reshape(n_rows, GRID_W, RG_SEGS, w)

        def shifted(d):
            pad = jnp.zeros((n_rows, abs(d), RG_SEGS, w), F32)
            if d > 0:
                return jnp.concatenate([pad, u4[:, :GRID_W - d]], axis=1)
            return jnp.concatenate([u4[:, -d:], pad], axis=1)

        uc = (cw_ref[2:3, :] * u4 + cb_ref[...] + cw_ref[0:1, :] * shifted(2)
              + cw_ref[1:2, :] * shifted(1) + cw_ref[3:4, :] * shifted(-1)).reshape(tb, w)
        a, x = coeffs(uc)
        ac_scr[...] = a
        hl_scr[...] = x
    else:
        row = lax.broadcasted_iota(jnp.int32, (tb, w), 0)
        uc = cw_ref[2:3, :] * u + cb_ref[...]
        uc = uc + cw_ref[0:1, :] * jnp.where(row >= 2, pltpu.roll(u, 2, 0), 0.0)
        uc = uc + cw_ref[1:2, :] * jnp.where(row >= 1, pltpu.roll(u, 1, 0), 0.0)
        uc = uc + cw_ref[3:4, :] * jnp.where(row < tb - 1, pltpu.roll(u, tb - 1, 0), 0.0)
        a, x = coeffs(uc)
        interleave(a, a_scr, ac_scr)
        interleave(x, x_scr, hl_scr)

    def pass1(ii, carry):
        i = (seg - 1 - ii) if reverse else ii
        h, cum = carry
        rows = pl.ds(pl.multiple_of(i * RG_SEGS, RG_SEGS), RG_SEGS)
        ai = ac_scr[rows, :]
        h = ai * h + hl_scr[rows, :]
        cum = ai * cum
        hl_scr[rows, :] = h
        ac_scr[rows, :] = cum
        return h, cum

    h_end, a_tot = lax.fori_loop(
        0, seg, pass1, (jnp.zeros((RG_SEGS, w), F32), jnp.ones((RG_SEGS, w), F32)), unroll=4)

    cur = h_ref[...]
    per_seg = [None] * RG_SEGS
    for sgm in (range(RG_SEGS - 1, -1, -1) if reverse else range(RG_SEGS)):
        per_seg[sgm] = cur
        cur = h_end[sgm:sgm + 1, :] + a_tot[sgm:sgm + 1, :] * cur
    enter = jnp.concatenate(per_seg, axis=0)
    h_out = cur
    h_ref[...] = h_out

    def pass2(i, carry):
        rows = pl.ds(pl.multiple_of(i * RG_SEGS, RG_SEGS), RG_SEGS)
        hfull = hl_scr[rows, :] + ac_scr[rows, :] * enter
        for c in range(nsl):
            hs_scr[c, pl.ds(i, RG_SEGS, stride=pitch), :] = hfull[:, c * LANES:(c + 1) * LANES]
        return carry

    lax.fori_loop(0, seg, pass2, 0, unroll=4)

    for sgm in range(RG_SEGS):
        rows = slice(sgm * seg, (sgm + 1) * seg)
        hn = jnp.concatenate([hs_scr[c, sgm * pitch:sgm * pitch + seg, :] for c in range(nsl)],
                             axis=1)
        o_ref[0, rows, :] = hn

    @pl.when(j == pl.num_programs(1) - 1)
    def _():
        hout_ref[0] = h_out


def _rg_scan(p, conv_w, conv_b, wg, bg, ls, h0, reverse, row_conv, tb):
    b, t, _ = p.shape
    nb = t // tb
    w = RG_WIDTH

    def blk(j):
        return (nb - 1 - j) if reverse else j

    def col(cidx):
        return pl.BlockSpec((1, tb, w), lambda i, j: (i, blk(j), cidx))

    h_spec = pl.BlockSpec((1, 1, w), lambda i, j: (i, 0, 0))
    in_specs = [col(COL_U), _const_spec((4, w)), _const_spec((1, w)),
                _const_spec((w, 2 * w)), _const_spec((1, 2 * w)), _const_spec((1, w)), h_spec]
    args = [p, conv_w, conv_b.reshape(1, w), wg, bg.reshape(1, 2 * w), ls.reshape(1, w), h0]
    return pl.pallas_call(
        functools.partial(_rg_kernel, reverse=reverse, row_conv=row_conv, tb=tb),
        grid=(b, nb),
        in_specs=in_specs,
        out_specs=[pl.BlockSpec((1, tb, w), lambda i, j: (i, blk(j), 0)), h_spec],
        out_shape=[jax.ShapeDtypeStruct((b, t, w), F32), jax.ShapeDtypeStruct((b, 1, w), F32)],
        scratch_shapes=[pltpu.VMEM((w // LANES, tb + RG_SEGS * SUBLANES, LANES), F32)] * 3
        + [pltpu.VMEM((tb, w), F32)] * 2 + [pltpu.VMEM((1, w), F32)]
        + [pltpu.VMEM((tb + 2 * SUBLANES, w), F32)],
        compiler_params=_params(("parallel", "arbitrary")),
        name="rglru_bwd" if reverse else "rglru_fwd",
    )(*args)


def _out_ffn_kernel(x_ref, os_ref, og_ref, hf_ref, hb_ref, gate_ref, m_ref, g_ref, gn_ref,
                    wo_ref, wg_ref, wu_ref, wd_ref, o_ref):
    osum = os_ref[0]
    heads = [_rms(osum[:, h * HG_DK:(h + 1) * HG_DK], gn_ref[...]) for h in range(HG_HEADS)]
    hg = (jnp.concatenate(heads, axis=1) * _silu(og_ref[0])).astype(BF16)
    rg = ((hf_ref[0] + hb_ref[0]) * _gelu_tanh(gate_ref[0])).astype(BF16)
    o = jnp.dot(hg, wo_ref[:HG_WIDTH, :], preferred_element_type=F32)
    o = o + jnp.dot(rg, wo_ref[HG_WIDTH:, :], preferred_element_type=F32)
    x1 = x_ref[0] + m_ref[0, 2:3, :] * _rms(o, g_ref[1:2, :])
    f = (_rms(x1, g_ref[2:3, :]) * (1.0 + m_ref[0, 4:5, :]) + m_ref[0, 3:4, :]).astype(BF16)
    gate = jnp.dot(f, wg_ref[...], preferred_element_type=F32)
    up = jnp.dot(f, wu_ref[...], preferred_element_type=F32)
    ff = jnp.dot((_silu(gate) * up).astype(BF16), wd_ref[...], preferred_element_type=F32)
    o_ref[0] = x1 + m_ref[0, 5:6, :] * _rms(ff, g_ref[3:4, :])


def _out_ffn(x, osum, p, hf, hb, m, g, gnorm, wo, wg, wu, wd, tm):
    b, t, d = x.shape
    mb = m.shape[0]
    dff = wg.shape[1]
    mixw = wo.shape[0]

    def resident(shape):
        return pl.BlockSpec(shape, lambda i, j: (0, 0), pipeline_mode=pl.Buffered(1))

    def tok(width):
        return pl.BlockSpec((1, tm, width), lambda i, j: (i, j, 0))

    return pl.pallas_call(
        _out_ffn_kernel,
        grid=(b, t // tm),
        in_specs=[tok(d), tok(HG_WIDTH),
                  pl.BlockSpec((1, tm, HG_WIDTH), lambda i, j: (i, j, COL_G)),
                  tok(RG_WIDTH), tok(RG_WIDTH),
                  pl.BlockSpec((1, tm, RG_WIDTH), lambda i, j: (i, j, COL_GATE)),
                  pl.BlockSpec((1, N_MOD, d), lambda i, j: (i if mb > 1 else 0, 0, 0)),
                  _const_spec((4, d)), _const_spec((1, HG_DK)),
                  resident((mixw, d)), resident((d, dff)), resident((d, dff)),
                  resident((dff, d))],
        out_specs=tok(d),
        out_shape=jax.ShapeDtypeStruct((b, t, d), F32),
        compiler_params=_params(("parallel", "parallel")),
        name="out_ffn",
    )(x, osum, p, hf, hb, p, m, g, gnorm.reshape(1, HG_DK), wo, wg, wu, wd)


def _block_diag(wb):
    nb, n, _ = wb.shape
    eye = jnp.eye(nb, dtype=wb.dtype)
    return jnp.einsum("nij,nm->nimj", wb, eye).reshape(nb * n, nb * n)


def _largest_tile(t, cap, step):
    best = step
    for cand in range(step, min(t, cap) + 1, step):
        if t % cand == 0:
            best = cand
    return best


def kernel(x, c, ctx, c_ctx, w_mod, b_mod, norm_g, w_in, hg_lb_logits, hg_gnorm, rg_conv_w, rg_conv_b, rg_w_a, rg_b_a, rg_w_x, rg_b_x, rg_lambda, w_out, w_ffn_gate, w_ffn_up, w_ffn_down):
    depth = w_mod.shape[0]
    bsz, seq, d = x.shape
    n_ctx = ctx.shape[1]

    lb_all = jnp.cumsum(jax.nn.softmax(hg_lb_logits.astype(F32), axis=0), axis=0)
    lb_all = lb_all - lb_all[0]
    log_sig = jax.nn.log_sigmoid(rg_lambda.astype(F32))

    rows = -(-(bsz + 1) // SUBLANES) * SUBLANES
    cvec = jnp.zeros((rows, d), F32).at[:bsz].set(c).at[bsz].set(c_ctx)
    mod = _modulation(cvec, w_mod, b_mod).reshape(depth, rows, N_MOD, d)

    tm_lat = _largest_tile(seq, 512, 128)
    tm_ctx = _largest_tile(n_ctx, 512, 128)
    tb_hg = _largest_tile(seq, 1024, CHUNK)
    tb_rg = _largest_tile(seq, RG_SEGS * GRID_W, RG_SEGS * GRID_W)

    x_lat, x_ctx = x, ctx
    for layer in range(depth):
        with_ctx = layer < depth - 1
        m_l = mod[layer, :bsz]
        m_c = mod[layer, bsz:bsz + 1]
        g = norm_g[layer]
        w_in_bf = w_in[layer].astype(BF16)
        p_l = _in_proj(x_lat, m_l, g[0], w_in_bf, tm_lat)
        p_c = _in_proj(x_ctx, m_c, g[0], w_in_bf, tm_ctx)

        s0 = jnp.zeros((bsz, HG_HEADS, HG_DK, HG_DK), F32)
        of_c, s_f = _hg_scan(p_c, lb_all[layer, 0], s0, False, n_ctx)
        of_l, _ = _hg_scan(p_l, lb_all[layer, 0], s_f, False, tb_hg)
        os_c, s_b = _hg_scan(p_c, lb_all[layer, 1], s0, True, n_ctx, of=of_c)
        os_l, _ = _hg_scan(p_l, lb_all[layer, 1], s_b, True, tb_hg, of=of_l)

        h0 = jnp.zeros((bsz, 1, RG_WIDTH), F32)
        cw = rg_conv_w[layer].astype(F32)
        cb = rg_conv_b[layer].astype(F32)
        wgs = [jnp.concatenate([_block_diag(rg_w_a[layer, dd]), _block_diag(rg_w_x[layer, dd])],
                               axis=1).astype(BF16) for dd in range(2)]
        bgs = [jnp.concatenate([rg_b_a[layer, dd], rg_b_x[layer, dd]]).astype(F32)
               for dd in range(2)]
        hf_c, h_f = _rg_scan(p_c, cw, cb, wgs[0], bgs[0], log_sig[layer, 0], h0, False, False,
                             n_ctx)
        hf_l, _ = _rg_scan(p_l, cw, cb, wgs[0], bgs[0], log_sig[layer, 0], h_f, False, True,
                           tb_rg)
        hb_c, h_b = _rg_scan(p_c, cw, cb, wgs[1], bgs[1], log_sig[layer, 1], h0, True, False,
                             n_ctx)
        hb_l, _ = _rg_scan(p_l, cw, cb, wgs[1], bgs[1], log_sig[layer, 1], h_b, True, True,
                           tb_rg)

        wo = w_out[layer].astype(BF16)
        wg_ = w_ffn_gate[layer].astype(BF16)
        wu_ = w_ffn_up[layer].astype(BF16)
        wd_ = w_ffn_down[layer].astype(BF16)
        gn = hg_gnorm[layer].astype(F32)
        x_lat = _out_ffn(x_lat, os_l, p_l, hf_l, hb_l, m_l, g, gn, wo, wg_, wu_, wd_, tm_lat)
        if with_ctx:
            x_ctx = _out_ffn(x_ctx, os_c, p_c, hf_c, hb_c, m_c, g, gn, wo, wg_, wu_, wd_,
                             tm_ctx)
    return x_lat
```

```python
import functools

import numpy as np
import jax
import jax.numpy as jnp
from jax import lax
from jax.experimental import pallas as pl
from jax.experimental.pallas import tpu as pltpu

F32 = jnp.float32
BF16 = jnp.bfloat16

EPS = 1e-6
N_MOD = 6
HG_HEADS = 4
HG_DK = 128
HG_WIDTH = HG_HEADS * HG_DK
RG_WIDTH = 512
RG_BLOCKS = 8
RG_C = 8.0
GRID_W = 64
CHUNK = 128
LEVELS = (1, 2, 4, 8, 16, 32, 64)
SUBLANES = 8
LANES = 128
RG_SEGS = 16
LOG2E = 1.4426950408889634
VMEM_LIMIT = 56 * 1024 * 1024

COL_Q, COL_ZF, COL_ZB, COL_V, COL_G, COL_U, COL_GATE = range(7)


def _params(sem):
    return pltpu.CompilerParams(dimension_semantics=sem, vmem_limit_bytes=VMEM_LIMIT)


def _const_spec(shape):
    nd = len(shape)
    return pl.BlockSpec(shape, lambda *_: (0,) * nd)


def _sigmoid(x):
    return 0.5 * jnp.tanh(0.5 * x) + 0.5


def _silu(x):
    h = 0.5 * x
    return h + h * jnp.tanh(h)


def _rms(x, g):
    return x * lax.rsqrt(jnp.mean(x * x, axis=-1, keepdims=True) + EPS) * g


def _mod_kernel(c_ref, w_ref, b_ref, o_ref):
    a = _silu(c_ref[...])
    o_ref[0] = jnp.dot(a, w_ref[0], precision=lax.Precision.HIGHEST,
                       preferred_element_type=F32) + b_ref[0]


def _modulation(cvec, w_mod, b_mod):
    depth, d, n = w_mod.shape
    rows = cvec.shape[0]
    nb = 4
    bn = n // nb
    return pl.pallas_call(
        _mod_kernel,
        grid=(depth, nb),
        in_specs=[
            pl.BlockSpec((rows, d), lambda l, j: (0, 0)),
            pl.BlockSpec((1, d, bn), lambda l, j: (l, 0, j)),
            pl.BlockSpec((1, 1, bn), lambda l, j: (l, 0, j)),
        ],
        out_specs=pl.BlockSpec((1, rows, bn), lambda l, j: (l, 0, j)),
        out_shape=jax.ShapeDtypeStruct((depth, rows, n), F32),
        compiler_params=_params(("parallel", "parallel")),
        name="adaln_mod",
    )(cvec, w_mod, b_mod.reshape(depth, 1, n))


def _proj_kernel(x_ref, m_ref, g_ref, w_ref, o_ref):
    h = _rms(x_ref[0], g_ref[...]) * (1.0 + m_ref[0, 1:2, :]) + m_ref[0, 0:1, :]
    o_ref[0] = jnp.dot(h.astype(BF16), w_ref[...], preferred_element_type=F32)


def _in_proj(x, m, g0, w_in_bf, tm):
    b, t, d = x.shape
    n = w_in_bf.shape[1]
    mb = m.shape[0]
    return pl.pallas_call(
        _proj_kernel,
        grid=(b, t // tm),
        in_specs=[
            pl.BlockSpec((1, tm, d), lambda i, j: (i, j, 0)),
            pl.BlockSpec((1, N_MOD, d), lambda i, j: (i if mb > 1 else 0, 0, 0)),
            _const_spec((1, d)),
            pl.BlockSpec((d, n), lambda i, j: (0, 0), pipeline_mode=pl.Buffered(1)),
        ],
        out_specs=pl.BlockSpec((1, tm, n), lambda i, j: (i, j, 0)),
        out_shape=jax.ShapeDtypeStruct((b, t, n), F32),
        compiler_params=_params(("parallel", "parallel")),
        name="in_proj",
    )(x, m, g0.reshape(1, d), w_in_bf)


def _level_masks(reverse):
    t = np.arange(CHUNK)[:, None]
    s = np.arange(CHUNK)[None, :]
    if reverse:
        t, s = s, t
    masks = [t == s]
    for b in LEVELS:
        masks.append((t // b == s // b + 1) & ((s // b) % 2 == 0))
    return np.stack(masks).astype(np.float32)


def _hg_chunk(q_raw, z, v, lb, st_ref, mask_ref, reverse):
    n = CHUNK
    w = q_raw.shape[1]
    ng = n // SUBLANES
    q = _silu(q_raw)
    e = jnp.exp(-jnp.abs(z))
    r = 1.0 / (1.0 + e)
    pos = z >= 0
    sig = jnp.where(pos, r, e * r)
    sig_neg = jnp.where(pos, e * r, r)
    k = (1.0 - lb) * sig_neg
    f = lb + (1.0 - lb) * sig
    lf2 = jnp.log(f) * LOG2E

    sub = lax.broadcasted_iota(jnp.int32, (ng, SUBLANES, w), 1)

    def later(b):
        on = (sub & b) != 0
        return jnp.logical_not(on) if reverse else on

    g3 = lf2.reshape(ng, SUBLANES, w)
    pos_in_group = (SUBLANES - 1 - sub) if reverse else sub
    d = 1
    while d < SUBLANES:
        sh = pltpu.roll(g3, (SUBLANES - d) if reverse else d, 1)
        g3 = g3 + jnp.where(pos_in_group >= d, sh, 0.0)
        d *= 2
    edge = 0 if reverse else SUBLANES - 1
    groups = [None] * ng
    carry = None
    for g in (range(ng - 1, -1, -1) if reverse else range(ng)):
        blk = g3[g] if carry is None else g3[g] + carry
        groups[g] = blk
        carry = blk[edge:edge + 1, :]
    lc = jnp.concatenate(groups, axis=0)
    lc3 = lc.reshape(ng, SUBLANES, w)
    q3 = q.reshape(ng, SUBLANES, w)
    k3 = k.reshape(ng, SUBLANES, w)

    acc = [None] * HG_HEADS
    vb = v.astype(BF16)

    def add_scores(lvl, a_bf, b_bf):
        for h in range(HG_HEADS):
            sl = slice(h * HG_DK, (h + 1) * HG_DK)
            sc = lax.dot_general(a_bf[:, sl], b_bf[:, sl], (((1,), (1,)), ((), ())),
                                 preferred_element_type=F32).astype(BF16) * mask_ref[lvl]
            acc[h] = sc if acc[h] is None else acc[h] + sc

    def bcast_row(r):
        return jnp.broadcast_to(lc3[:, r:r + 1, :], (ng, SUBLANES, w))

    add_scores(0, q.astype(BF16), k.astype(BF16))
    for lvl, b in enumerate(LEVELS):
        if b == 1:
            xb = jnp.where(later(1), q3 * f.reshape(ng, SUBLANES, w), k3)
        elif b < SUBLANES:
            rows = [sb * 2 * b + (b if reverse else b - 1) for sb in range(SUBLANES // (2 * b))]
            anchor = bcast_row(rows[0])
            for i in range(1, len(rows)):
                anchor = jnp.where(sub >= i * 2 * b, bcast_row(rows[i]), anchor)
            wgt = jnp.exp2(-jnp.abs(lc3 - anchor))
            xb = jnp.where(later(b), q3, k3) * wgt
        else:
            pieces = []
            for sb in range(n // (2 * b)):
                lo, mid, hi = sb * 2 * b, sb * 2 * b + b, (sb + 1) * 2 * b
                if reverse:
                    anc = lc[mid:mid + 1, :]
                    pieces.append(q[lo:mid] * jnp.exp2(lc[lo:mid] - anc))
                    pieces.append(k[mid:hi] * jnp.exp2(anc - lc[mid:hi]))
                else:
                    anc = lc[mid - 1:mid, :]
                    pieces.append(k[lo:mid] * jnp.exp2(anc - lc[lo:mid]))
                    pieces.append(q[mid:hi] * jnp.exp2(lc[mid:hi] - anc))
            xb = jnp.concatenate(pieces, axis=0)
        xb = xb.reshape(n, w).astype(BF16)
        add_scores(lvl + 1, xb, xb)

    last = lc[0:1, :] if reverse else lc[n - 1:n, :]
    q_in = (q * jnp.exp2(lc)).astype(BF16)
    k_out = (k * jnp.exp2(last - lc)).astype(BF16)
    total = jnp.exp2(last)
    outs = []
    for h in range(HG_HEADS):
        sl = slice(h * HG_DK, (h + 1) * HG_DK)
        st = st_ref[h]
        o = jnp.dot(acc[h], vb[:, sl], preferred_element_type=F32)
        o = o + lax.dot_general(q_in[:, sl], st.astype(BF16), (((1,), (1,)), ((), ())),
                                preferred_element_type=F32)
        st_ref[h] = st * total[:, sl] + lax.dot_general(
            vb[:, sl], k_out[:, sl], (((0,), (0,)), ((), ())), preferred_element_type=F32)
        outs.append(o)
    return jnp.concatenate(outs, axis=1)


def _hg_kernel(*refs, reverse, readout, n_chunks):
    if readout:
        (q_ref, z_ref, v_ref, lb_ref, s0_ref, mask_ref, of_ref,
         o_ref, sout_ref, st_ref) = refs
    else:
        q_ref, z_ref, v_ref, lb_ref, s0_ref, mask_ref, o_ref, sout_ref, st_ref = refs
    j = pl.program_id(1)

    @pl.when(j == 0)
    def _():
        st_ref[...] = s0_ref[0]

    lb = lb_ref[...]

    def body(ci, carry):
        c = (n_chunks - 1 - ci) if reverse else ci
        rows = pl.ds(pl.multiple_of(c * CHUNK, CHUNK), CHUNK)
        o = _hg_chunk(q_ref[0, rows, :], z_ref[0, rows, :], v_ref[0, rows, :], lb,
                      st_ref, mask_ref, reverse)
        if readout:
            o = o + of_ref[0, rows, :]
        o_ref[0, rows, :] = o
        return carry

    lax.fori_loop(0, n_chunks, body, 0, unroll=4)

    @pl.when(j == pl.num_programs(1) - 1)
    def _():
        sout_ref[0] = st_ref[...]


def _hg_scan(p, lb, s0, reverse, tb, of=None):
    b, t, _ = p.shape
    nb = t // tb
    readout = of is not None
    masks = jnp.asarray(_level_masks(reverse), dtype=BF16)

    def blk(j):
        return (nb - 1 - j) if reverse else j

    def col(cidx):
        return pl.BlockSpec((1, tb, HG_WIDTH), lambda i, j: (i, blk(j), cidx))

    state_spec = pl.BlockSpec((1, HG_HEADS, HG_DK, HG_DK), lambda i, j: (i, 0, 0, 0))
    in_specs = [col(COL_Q), col(COL_ZB if reverse else COL_ZF), col(COL_V),
                _const_spec((1, HG_WIDTH)), state_spec,
                _const_spec((len(LEVELS) + 1, CHUNK, CHUNK))]
    args = [p, p, p, lb.reshape(1, HG_WIDTH), s0, masks]
    if readout:
        in_specs += [pl.BlockSpec((1, tb, HG_WIDTH), lambda i, j: (i, blk(j), 0))]
        args += [of]
    return pl.pallas_call(
        functools.partial(_hg_kernel, reverse=reverse, readout=readout,
                          n_chunks=tb // CHUNK),
        grid=(b, nb),
        in_specs=in_specs,
        out_specs=[pl.BlockSpec((1, tb, HG_WIDTH), lambda i, j: (i, blk(j), 0)), state_spec],
        out_shape=[jax.ShapeDtypeStruct((b, t, HG_WIDTH), F32),
                   jax.ShapeDtypeStruct((b, HG_HEADS, HG_DK, HG_DK), F32)],
        scratch_shapes=[pltpu.VMEM((HG_HEADS, HG_DK, HG_DK), F32)],
        compiler_params=_params(("parallel", "arbitrary")),
        name="hgrn2_bwd" if reverse else "hgrn2_fwd",
    )(*args)


def _gelu_tanh(x):
    return 0.5 * x * (1.0 + jnp.tanh(0.7978845608028654 * (x + 0.044715 * (x * x * x))))


def _rg_kernel(*refs, reverse, row_conv, tb):
    (u_ref, cw_ref, cb_ref, wg_ref, bg_ref, ls_ref, h0_ref,
     o_ref, hout_ref, a_scr, x_scr, hs_scr, hl_scr, ac_scr, h_ref, up_scr) = refs
    j = pl.program_id(1)

    @pl.when(j == 0)
    def _():
        h_ref[...] = h0_ref[0]

    u = u_ref[0]
    w = u.shape[1]
    seg = tb // RG_SEGS
    pitch = seg + SUBLANES
    nsl = w // LANES

    def interleave(val, stage_scr, dst_scr):
        for c in range(nsl):
            for sgm in range(RG_SEGS):
                stage_scr[c, sgm * pitch:sgm * pitch + seg, :] = val[sgm * seg:(sgm + 1) * seg,
                                                                     c * LANES:(c + 1) * LANES]

        def body(i, carry):
            rows = pl.ds(pl.multiple_of(i * RG_SEGS, RG_SEGS), RG_SEGS)
            for c in range(nsl):
                dst_scr[rows, c * LANES:(c + 1) * LANES] = stage_scr[
                    c, pl.ds(i, RG_SEGS, stride=pitch), :]
            return carry

        lax.fori_loop(0, seg, body, 0, unroll=8)

    def coeffs(uc):
        gates = jnp.dot(uc.astype(BF16), wg_ref[...], preferred_element_type=F32) + bg_ref[...]
        r = _sigmoid(gates[:, :w])
        i = _sigmoid(gates[:, w:])
        a = jnp.exp2(r * ((RG_C * LOG2E) * ls_ref[...]))
        y = 1.0 - a * a
        return a, jnp.where(y > 0.0, y * lax.rsqrt(y), 0.0) * (i * uc)

    if row_conv:
        interleave(u, a_scr, up_scr)
        n_rows = seg // GRID_W
        u4 = up_scr[0:tb, :].reshape(n_rows, GRID_W, RG_SEGS, w)

        def shifted(d):
            pad = jnp.zeros((n_rows, abs(d), RG_SEGS, w), F32)
            if d > 0:
                return jnp.concatenate([pad, u4[:, :GRID_W - d]], axis=1)
            return jnp.concatenate([u4[:, -d:], pad], axis=1)

        uc = (cw_ref[2:3, :] * u4 + cb_ref[...] + cw_ref[0:1, :] * shifted(2)
              + cw_ref[1:2, :] * shifted(1) + cw_ref[3:4, :] * shifted(-1)).reshape(tb, w)
        a, x = coeffs(uc)
        ac_scr[...] = a
        hl_scr[...] = x
    else:
        row = lax.broadcasted_iota(jnp.int32, (tb, w), 0)
        uc = cw_ref[2:3, :] * u + cb_ref[...]
        uc = uc + cw_ref[0:1, :] * jnp.where(row >= 2, pltpu.roll(u, 2, 0), 0.0)
        uc = uc + cw_ref[1:2, :] * jnp.where(row >= 1, pltpu.roll(u, 1, 0), 0.0)
        uc = uc + cw_ref[3:4, :] * jnp.where(row < tb - 1, pltpu.roll(u, tb - 1, 0), 0.0)
        a, x = coeffs(uc)
        interleave(a, a_scr, ac_scr)
        interleave(x, x_scr, hl_scr)

    def pass1(ii, carry):
        i = (seg - 1 - ii) if reverse else ii
        h, cum = carry
        rows = pl.ds(pl.multiple_of(i * RG_SEGS, RG_SEGS), RG_SEGS)
        ai = ac_scr[rows, :]
        h = ai * h + hl_scr[rows, :]
        cum = ai * cum
        hl_scr[rows, :] = h
        ac_scr[rows, :] = cum
        return h, cum

    h_end, a_tot = lax.fori_loop(
        0, seg, pass1, (jnp.zeros((RG_SEGS, w), F32), jnp.ones((RG_SEGS, w), F32)), unroll=4)

    cur = h_ref[...]
    per_seg = [None] * RG_SEGS
    for sgm in (range(RG_SEGS - 1, -1, -1) if reverse else range(RG_SEGS)):
        per_seg[sgm] = cur
        cur = h_end[sgm:sgm + 1, :] + a_tot[sgm:sgm + 1, :] * cur
    enter = jnp.concatenate(per_seg, axis=0)
    h_out = cur
    h_ref[...] = h_out

    def pass2(i, carry):
        rows = pl.ds(pl.multiple_of(i * RG_SEGS, RG_SEGS), RG_SEGS)
        hfull = hl_scr[rows, :] + ac_scr[rows, :] * enter
        for c in range(nsl):
            hs_scr[c, pl.ds(i, RG_SEGS, stride=pitch), :] = hfull[:, c * LANES:(c + 1) * LANES]
        return carry

    lax.fori_loop(0, seg, pass2, 0, unroll=4)

    for sgm in range(RG_SEGS):
        rows = slice(sgm * seg, (sgm + 1) * seg)
        hn = jnp.concatenate([hs_scr[c, sgm * pitch:sgm * pitch + seg, :] for c in range(nsl)],
                             axis=1)
        o_ref[0, rows, :] = hn

    @pl.when(j == pl.num_programs(1) - 1)
    def _():
        hout_ref[0] = h_out


def _rg_scan(p, conv_w, conv_b, wg, bg, ls, h0, reverse, row_conv, tb):
    b, t, _ = p.shape
    nb = t // tb
    w = RG_WIDTH

    def blk(j):
        return (nb - 1 - j) if reverse else j

    def col(cidx):
        return pl.BlockSpec((1, tb, w), lambda i, j: (i, blk(j), cidx))

    h_spec = pl.BlockSpec((1, 1, w), lambda i, j: (i, 0, 0))
    in_specs = [col(COL_U), _const_spec((4, w)), _const_spec((1, w)),
                _const_spec((w, 2 * w)), _const_spec((1, 2 * w)), _const_spec((1, w)), h_spec]
    args = [p, conv_w, conv_b.reshape(1, w), wg, bg.reshape(1, 2 * w), ls.reshape(1, w), h0]
    return pl.pallas_call(
        functools.partial(_rg_kernel, reverse=reverse, row_conv=row_conv, tb=tb),
        grid=(b, nb),
        in_specs=in_specs,
        out_specs=[pl.BlockSpec((1, tb, w), lambda i, j: (i, blk(j), 0)), h_spec],
        out_shape=[jax.ShapeDtypeStruct((b, t, w), F32), jax.ShapeDtypeStruct((b, 1, w), F32)],
        scratch_shapes=[pltpu.VMEM((w // LANES, tb + RG_SEGS * SUBLANES, LANES), F32)] * 3
        + [pltpu.VMEM((tb, w), F32)] * 2 + [pltpu.VMEM((1, w), F32)]
        + [pltpu.VMEM((tb + 2 * SUBLANES, w), F32)],
        compiler_params=_params(("parallel", "arbitrary")),
        name="rglru_bwd" if reverse else "rglru_fwd",
    )(*args)


def _out_ffn_kernel(x_ref, os_ref, og_ref, hf_ref, hb_ref, gate_ref, m_ref, g_ref, gn_ref,
                    wo_ref, wg_ref, wu_ref, wd_ref, o_ref):
    osum = os_ref[0]
    heads = [_rms(osum[:, h * HG_DK:(h + 1) * HG_DK], gn_ref[...]) for h in range(HG_HEADS)]
    hg = (jnp.concatenate(heads, axis=1) * _silu(og_ref[0])).astype(BF16)
    rg = ((hf_ref[0] + hb_ref[0]) * _gelu_tanh(gate_ref[0])).astype(BF16)
    o = jnp.dot(hg, wo_ref[:HG_WIDTH, :], preferred_element_type=F32)
    o = o + jnp.dot(rg, wo_ref[HG_WIDTH:, :], preferred_element_type=F32)
    x1 = x_ref[0] + m_ref[0, 2:3, :] * _rms(o, g_ref[1:2, :])
    f = (_rms(x1, g_ref[2:3, :]) * (1.0 + m_ref[0, 4:5, :]) + m_ref[0, 3:4, :]).astype(BF16)
    gate = jnp.dot(f, wg_ref[...], preferred_element_type=F32)
    up = jnp.dot(f, wu_ref[...], preferred_element_type=F32)
    ff = jnp.dot((_silu(gate) * up).astype(BF16), wd_ref[...], preferred_element_type=F32)
    o_ref[0] = x1 + m_ref[0, 5:6, :] * _rms(ff, g_ref[3:4, :])


def _out_ffn(x, osum, p, hf, hb, m, g, gnorm, wo, wg, wu, wd, tm):
    b, t, d = x.shape
    mb = m.shape[0]
    dff = wg.shape[1]
    mixw = wo.shape[0]

    def resident(shape):
        return pl.BlockSpec(shape, lambda i, j: (0, 0), pipeline_mode=pl.Buffered(1))

    def tok(width):
        return pl.BlockSpec((1, tm, width), lambda i, j: (i, j, 0))

    return pl.pallas_call(
        _out_ffn_kernel,
        grid=(b, t // tm),
        in_specs=[tok(d), tok(HG_WIDTH),
                  pl.BlockSpec((1, tm, HG_WIDTH), lambda i, j: (i, j, COL_G)),
                  tok(RG_WIDTH), tok(RG_WIDTH),
                  pl.BlockSpec((1, tm, RG_WIDTH), lambda i, j: (i, j, COL_GATE)),
                  pl.BlockSpec((1, N_MOD, d), lambda i, j: (i if mb > 1 else 0, 0, 0)),
                  _const_spec((4, d)), _const_spec((1, HG_DK)),
                  resident((mixw, d)), resident((d, dff)), resident((d, dff)),
                  resident((dff, d))],
        out_specs=tok(d),
        out_shape=jax.ShapeDtypeStruct((b, t, d), F32),
        compiler_params=_params(("parallel", "parallel")),
        name="out_ffn",
    )(x, osum, p, hf, hb, p, m, g, gnorm.reshape(1, HG_DK), wo, wg, wu, wd)


def _block_diag(wb):
    nb, n, _ = wb.shape
    eye = jnp.eye(nb, dtype=wb.dtype)
    return jnp.einsum("nij,nm->nimj", wb, eye).reshape(nb * n, nb * n)


def _largest_tile(t, cap, step):
    best = step
    for cand in range(step, min(t, cap) + 1, step):
        if t % cand == 0:
            best = cand
    return best


def kernel(x, c, ctx, c_ctx, w_mod, b_mod, norm_g, w_in, hg_lb_logits, hg_gnorm, rg_conv_w, rg_conv_b, rg_w_a, rg_b_a, rg_w_x, rg_b_x, rg_lambda, w_out, w_ffn_gate, w_ffn_up, w_ffn_down):
    depth = w_mod.shape[0]
    bsz, seq, d = x.shape
    n_ctx = ctx.shape[1]

    lb_all = jnp.cumsum(jax.nn.softmax(hg_lb_logits.astype(F32), axis=0), axis=0)
    lb_all = lb_all - lb_all[0]
    log_sig = jax.nn.log_sigmoid(rg_lambda.astype(F32))

    rows = -(-(bsz + 1) // SUBLANES) * SUBLANES
    cvec = jnp.zeros((rows, d), F32).at[:bsz].set(c).at[bsz].set(c_ctx)
    mod = _modulation(cvec, w_mod, b_mod).reshape(depth, rows, N_MOD, d)

    tm_lat = _largest_tile(seq, 512, 128)
    tm_ctx = _largest_tile(n_ctx, 512, 128)
    tb_hg = _largest_tile(seq, 1024, CHUNK)
    tb_rg = _largest_tile(seq, RG_SEGS * GRID_W, RG_SEGS * GRID_W)

    x_lat, x_ctx = x, ctx
    for layer in range(depth):
        with_ctx = layer < depth - 1
        m_l = mod[layer, :bsz]
        m_c = mod[layer, bsz:bsz + 1]
        g = norm_g[layer]
        w_in_bf = w_in[layer].astype(BF16)
        p_l = _in_proj(x_lat, m_l, g[0], w_in_bf, tm_lat)
        p_c = _in_proj(x_ctx, m_c, g[0], w_in_bf, tm_ctx)

        s0 = jnp.zeros((bsz, HG_HEADS, HG_DK, HG_DK), F32)
        of_c, s_f = _hg_scan(p_c, lb_all[layer, 0], s0, False, n_ctx)
        of_l, _ = _hg_scan(p_l, lb_all[layer, 0], s_f, False, tb_hg)
        os_c, s_b = _hg_scan(p_c, lb_all[layer, 1], s0, True, n_ctx, of=of_c)
        os_l, _ = _hg_scan(p_l, lb_all[layer, 1], s_b, True, tb_hg, of=of_l)

        h0 = jnp.zeros((bsz, 1, RG_WIDTH), F32)
        cw = rg_conv_w[layer].astype(F32)
        cb = rg_conv_b[layer].astype(F32)
        wgs = [jnp.concatenate([_block_diag(rg_w_a[layer, dd]), _block_diag(rg_w_x[layer, dd])],
                               axis=1).astype(BF16) for dd in range(2)]
        bgs = [jnp.concatenate([rg_b_a[layer, dd], rg_b_x[layer, dd]]).astype(F32)
               for dd in range(2)]
        hf_c, h_f = _rg_scan(p_c, cw, cb, wgs[0], bgs[0], log_sig[layer, 0], h0, False, False,
                             n_ctx)
        hf_l, _ = _rg_scan(p_l, cw, cb, wgs[0], bgs[0], log_sig[layer, 0], h_f, False, True,
                           tb_rg)
        hb_c, h_b = _rg_scan(p_c, cw, cb, wgs[1], bgs[1], log_sig[layer, 1], h0, True, False,
                             n_ctx)
        hb_l, _ = _rg_scan(p_l, cw, cb, wgs[1], bgs[1], log_sig[layer, 1], h_b, True, True,
                           tb_rg)

        wo = w_out[layer].astype(BF16)
        wg_ = w_ffn_gate[layer].astype(BF16)
        wu_ = w_ffn_up[layer].astype(BF16)
        wd_ = w_ffn_down[layer].astype(BF16)
        gn = hg_gnorm[layer].astype(F32)
        x_lat = _out_ffn(x_lat, os_l, p_l, hf_l, hb_l, m_l, g, gn, wo, wg_, wu_, wd_, tm_lat)
        if with_ctx:
            x_ctx = _out_ffn(x_ctx, os_c, p_c, hf_c, hb_c, m_c, g, gn, wo, wg_, wu_, wd_,
                             tm_ctx)
    return x_lat
```

```python
import functools

import numpy as np
import jax
import jax.numpy as jnp
from jax import lax
from jax.experimental import pallas as pl
from jax.experimental.pallas import tpu as pltpu

F32 = jnp.float32
BF16 = jnp.bfloat16

EPS = 1e-6
N_MOD = 6
HG_HEADS = 4
HG_DK = 128
HG_WIDTH = HG_HEADS * HG_DK
RG_WIDTH = 512
RG_BLOCKS = 8
RG_C = 8.0
GRID_W = 64
CHUNK = 128
LEVELS = (1, 2, 4, 8, 16, 32, 64)
SUBLANES = 8
LANES = 128
RG_SEGS = 16
LOG2E = 1.4426950408889634
VMEM_LIMIT = 56 * 1024 * 1024

COL_Q, COL_ZF, COL_ZB, COL_V, COL_G, COL_U, COL_GATE = range(7)


def _params(sem):
    return pltpu.CompilerParams(dimension_semantics=sem, vmem_limit_bytes=VMEM_LIMIT)


def _const_spec(shape):
    nd = len(shape)
    return pl.BlockSpec(shape, lambda *_: (0,) * nd)


def _sigmoid(x):
    return 0.5 * jnp.tanh(0.5 * x) + 0.5


def _silu(x):
    h = 0.5 * x
    return h + h * jnp.tanh(h)


def _rms(x, g):
    return x * lax.rsqrt(jnp.mean(x * x, axis=-1, keepdims=True) + EPS) * g


def _mod_kernel(c_ref, w_ref, b_ref, o_ref):
    a = _silu(c_ref[...])
    o_ref[0] = jnp.dot(a, w_ref[0], precision=lax.Precision.HIGHEST,
                       preferred_element_type=F32) + b_ref[0]


def _modulation(cvec, w_mod, b_mod):
    depth, d, n = w_mod.shape
    rows = cvec.shape[0]
    nb = 4
    bn = n // nb
    return pl.pallas_call(
        _mod_kernel,
        grid=(depth, nb),
        in_specs=[
            pl.BlockSpec((rows, d), lambda l, j: (0, 0)),
            pl.BlockSpec((1, d, bn), lambda l, j: (l, 0, j)),
            pl.BlockSpec((1, 1, bn), lambda l, j: (l, 0, j)),
        ],
        out_specs=pl.BlockSpec((1, rows, bn), lambda l, j: (l, 0, j)),
        out_shape=jax.ShapeDtypeStruct((depth, rows, n), F32),
        compiler_params=_params(("parallel", "parallel")),
        name="adaln_mod",
    )(cvec, w_mod, b_mod.reshape(depth, 1, n))


def _proj_kernel(x_ref, m_ref, g_ref, w_ref, o_ref):
    h = _rms(x_ref[0], g_ref[...]) * (1.0 + m_ref[0, 1:2, :]) + m_ref[0, 0:1, :]
    o_ref[0] = jnp.dot(h.astype(BF16), w_ref[...], preferred_element_type=F32)


def _in_proj(x, m, g0, w_in_bf, tm):
    b, t, d = x.shape
    n = w_in_bf.shape[1]
    mb = m.shape[0]
    return pl.pallas_call(
        _proj_kernel,
        grid=(b, t // tm),
        in_specs=[
            pl.BlockSpec((1, tm, d), lambda i, j: (i, j, 0)),
            pl.BlockSpec((1, N_MOD, d), lambda i, j: (i if mb > 1 else 0, 0, 0)),
            _const_spec((1, d)),
            pl.BlockSpec((d, n), lambda i, j: (0, 0), pipeline_mode=pl.Buffered(1)),
        ],
        out_specs=pl.BlockSpec((1, tm, n), lambda i, j: (i, j, 0)),
        out_shape=jax.ShapeDtypeStruct((b, t, n), F32),
        compiler_params=_params(("parallel", "parallel")),
        name="in_proj",
    )(x, m, g0.reshape(1, d), w_in_bf)


def _level_masks(reverse):
    t = np.arange(CHUNK)[:, None]
    s = np.arange(CHUNK)[None, :]
    if reverse:
        t, s = s, t
    masks = [t == s]
    for b in LEVELS:
        masks.append((t // b == s // b + 1) & ((s // b) % 2 == 0))
    return np.stack(masks).astype(np.float32)


def _level_signs(reverse):
    sub = np.arange(SUBLANES)[:, None]
    out = []
    for b in (2, 4):
        later = ((sub & b) == 0) if reverse else ((sub & b) != 0)
        out.append(np.broadcast_to(np.where(later, 1.0, -1.0), (SUBLANES, HG_WIDTH)))
    return np.stack(out).astype(np.float32)


def _hg_chunk(q_raw, z, v, lb, st_ref, mask_ref, sign_ref, reverse):
    n = CHUNK
    w = q_raw.shape[1]
    ng = n // SUBLANES
    q = _silu(q_raw)
    e = jnp.exp(-jnp.abs(z))
    r = 1.0 / (1.0 + e)
    pos = z >= 0
    sig = jnp.where(pos, r, e * r)
    sig_neg = jnp.where(pos, e * r, r)
    k = (1.0 - lb) * sig_neg
    f = lb + (1.0 - lb) * sig
    lf2 = jnp.log(f) * LOG2E

    sub = lax.broadcasted_iota(jnp.int32, (ng, SUBLANES, w), 1)

    def later(b):
        on = (sub & b) != 0
        return jnp.logical_not(on) if reverse else on

    g3 = lf2.reshape(ng, SUBLANES, w)
    pos_in_group = (SUBLANES - 1 - sub) if reverse else sub
    d = 1
    while d < SUBLANES:
        sh = pltpu.roll(g3, (SUBLANES - d) if reverse else d, 1)
        g3 = g3 + jnp.where(pos_in_group >= d, sh, 0.0)
        d *= 2
    edge = 0 if reverse else SUBLANES - 1
    groups = [None] * ng
    carry = None
    for g in (range(ng - 1, -1, -1) if reverse else range(ng)):
        blk = g3[g] if carry is None else g3[g] + carry
        groups[g] = blk
        carry = blk[edge:edge + 1, :]
    lc = jnp.concatenate(groups, axis=0)
    lc3 = lc.reshape(ng, SUBLANES, w)
    q3 = q.reshape(ng, SUBLANES, w)
    k3 = k.reshape(ng, SUBLANES, w)

    acc = [None] * HG_HEADS
    vb = v.astype(BF16)

    def add_scores(lvl, a_bf, b_bf):
        for h in range(HG_HEADS):
            sl = slice(h * HG_DK, (h + 1) * HG_DK)
            sc = lax.dot_general(a_bf[:, sl], b_bf[:, sl], (((1,), (1,)), ((), ())),
                                 preferred_element_type=F32).astype(BF16) * mask_ref[lvl]
            acc[h] = sc if acc[h] is None else acc[h] + sc

    def bcast_row(r):
        return jnp.broadcast_to(lc3[:, r:r + 1, :], (ng, SUBLANES, w))

    add_scores(0, q.astype(BF16), k.astype(BF16))
    for lvl, b in enumerate(LEVELS):
        if b == 1:
            xb = jnp.where(later(1), q3 * f.reshape(ng, SUBLANES, w), k3)
        elif b < SUBLANES:
            rows = [sb * 2 * b + (b if reverse else b - 1) for sb in range(SUBLANES // (2 * b))]
            anchor = bcast_row(rows[0])
            for i in range(1, len(rows)):
                anchor = jnp.where(sub >= i * 2 * b, bcast_row(rows[i]), anchor)
            wgt = jnp.exp2((lc3 - anchor) * sign_ref[lvl - 1])
            xb = jnp.where(later(b), q3, k3) * wgt
        else:
            pieces = []
            for sb in range(n // (2 * b)):
                lo, mid, hi = sb * 2 * b, sb * 2 * b + b, (sb + 1) * 2 * b
                if reverse:
                    anc = lc[mid:mid + 1, :]
                    pieces.append(q[lo:mid] * jnp.exp2(lc[lo:mid] - anc))
                    pieces.append(k[mid:hi] * jnp.exp2(anc - lc[mid:hi]))
                else:
                    anc = lc[mid - 1:mid, :]
                    pieces.append(k[lo:mid] * jnp.exp2(anc - lc[lo:mid]))
                    pieces.append(q[mid:hi] * jnp.exp2(lc[mid:hi] - anc))
            xb = jnp.concatenate(pieces, axis=0)
        xb = xb.reshape(n, w).astype(BF16)
        add_scores(lvl + 1, xb, xb)

    last = lc[0:1, :] if reverse else lc[n - 1:n, :]
    q_in = (q * jnp.exp2(lc)).astype(BF16)
    k_out = (k * jnp.exp2(last - lc)).astype(BF16)
    total = jnp.exp2(last)
    outs = []
    for h in range(HG_HEADS):
        sl = slice(h * HG_DK, (h + 1) * HG_DK)
        st = st_ref[h]
        o = jnp.dot(acc[h], vb[:, sl], preferred_element_type=F32)
        o = o + lax.dot_general(q_in[:, sl], st.astype(BF16), (((1,), (1,)), ((), ())),
                                preferred_element_type=F32)
        st_ref[h] = st * total[:, sl] + lax.dot_general(
            vb[:, sl], k_out[:, sl], (((0,), (0,)), ((), ())), preferred_element_type=F32)
        outs.append(o)
    return jnp.concatenate(outs, axis=1)


def _hg_kernel(*refs, reverse, readout, n_chunks):
    if readout:
        (q_ref, z_ref, v_ref, lb_ref, s0_ref, mask_ref, sign_ref, of_ref,
         o_ref, sout_ref, st_ref) = refs
    else:
        (q_ref, z_ref, v_ref, lb_ref, s0_ref, mask_ref, sign_ref,
         o_ref, sout_ref, st_ref) = refs
    j = pl.program_id(1)

    @pl.when(j == 0)
    def _():
        st_ref[...] = s0_ref[0]

    lb = lb_ref[...]

    def body(ci, carry):
        c = (n_chunks - 1 - ci) if reverse else ci
        rows = pl.ds(pl.multiple_of(c * CHUNK, CHUNK), CHUNK)
        o = _hg_chunk(q_ref[0, rows, :], z_ref[0, rows, :], v_ref[0, rows, :], lb,
                      st_ref, mask_ref, sign_ref, reverse)
        if readout:
            o = o + of_ref[0, rows, :]
        o_ref[0, rows, :] = o
        return carry

    lax.fori_loop(0, n_chunks, body, 0, unroll=4)

    @pl.when(j == pl.num_programs(1) - 1)
    def _():
        sout_ref[0] = st_ref[...]


def _hg_scan(p, lb, s0, reverse, tb, of=None):
    b, t, _ = p.shape
    nb = t // tb
    readout = of is not None
    masks = jnp.asarray(_level_masks(reverse), dtype=BF16)

    def blk(j):
        return (nb - 1 - j) if reverse else j

    def col(cidx):
        return pl.BlockSpec((1, tb, HG_WIDTH), lambda i, j: (i, blk(j), cidx))

    state_spec = pl.BlockSpec((1, HG_HEADS, HG_DK, HG_DK), lambda i, j: (i, 0, 0, 0))
    in_specs = [col(COL_Q), col(COL_ZB if reverse else COL_ZF), col(COL_V),
                _const_spec((1, HG_WIDTH)), state_spec,
                _const_spec((len(LEVELS) + 1, CHUNK, CHUNK)),
                _const_spec((2, SUBLANES, HG_WIDTH))]
    args = [p, p, p, lb.reshape(1, HG_WIDTH), s0, masks, jnp.asarray(_level_signs(reverse))]
    if readout:
        in_specs += [pl.BlockSpec((1, tb, HG_WIDTH), lambda i, j: (i, blk(j), 0))]
        args += [of]
    return pl.pallas_call(
        functools.partial(_hg_kernel, reverse=reverse, readout=readout,
                          n_chunks=tb // CHUNK),
        grid=(b, nb),
        in_specs=in_specs,
        out_specs=[pl.BlockSpec((1, tb, HG_WIDTH), lambda i, j: (i, blk(j), 0)), state_spec],
        out_shape=[jax.ShapeDtypeStruct((b, t, HG_WIDTH), F32),
                   jax.ShapeDtypeStruct((b, HG_HEADS, HG_DK, HG_DK), F32)],
        scratch_shapes=[pltpu.VMEM((HG_HEADS, HG_DK, HG_DK), F32)],
        compiler_params=_params(("parallel", "arbitrary")),
        name="hgrn2_bwd" if reverse else "hgrn2_fwd",
    )(*args)


def _gelu_tanh(x):
    return 0.5 * x * (1.0 + jnp.tanh(0.7978845608028654 * (x + 0.044715 * (x * x * x))))


def _rg_kernel(*refs, reverse, row_conv, tb):
    (u_ref, cw_ref, cb_ref, wg_ref, bg_ref, ls_ref, h0_ref,
     o_ref, hout_ref, a_scr, x_scr, hs_scr, hl_scr, ac_scr, h_ref, up_scr) = refs
    j = pl.program_id(1)

    @pl.when(j == 0)
    def _():
        h_ref[...] = h0_ref[0]

    u = u_ref[0]
    w = u.shape[1]
    seg = tb // RG_SEGS
    pitch = seg + SUBLANES
    nsl = w // LANES

    def interleave(val, stage_scr, dst_scr):
        for c in range(nsl):
            for sgm in range(RG_SEGS):
                stage_scr[c, sgm * pitch:sgm * pitch + seg, :] = val[sgm * seg:(sgm + 1) * seg,
                                                                     c * LANES:(c + 1) * LANES]

        def body(i, carry):
            rows = pl.ds(pl.multiple_of(i * RG_SEGS, RG_SEGS), RG_SEGS)
            for c in range(nsl):
                dst_scr[rows, c * LANES:(c + 1) * LANES] = stage_scr[
                    c, pl.ds(i, RG_SEGS, stride=pitch), :]
            return carry

        lax.fori_loop(0, seg, body, 0, unroll=8)

    def coeffs(uc):
        gates = jnp.dot(uc.astype(BF16), wg_ref[...], preferred_element_type=F32) + bg_ref[...]
        r = 0.5 * jnp.tanh(gates[:, :w]) + 0.5
        i = 0.5 * jnp.tanh(gates[:, w:]) + 0.5
        a = jnp.exp2(r * ((RG_C * LOG2E) * ls_ref[...]))
        y = 1.0 - a * a
        return a, jnp.where(y > 0.0, y * lax.rsqrt(y), 0.0) * (i * uc)

    if row_conv:
        interleave(u, a_scr, up_scr)
        n_rows = seg // GRID_W
        u4 = up_scr[0:tb, :].reshape(n_rows, GRID_W, RG_SEGS, w)

        def shifted(d):
            pad = jnp.zeros((n_rows, abs(d), RG_SEGS, w), F32)
            if d > 0:
                return jnp.concatenate([pad, u4[:, :GRID_W - d]], axis=1)
            return jnp.concatenate([u4[:, -d:], pad], axis=1)

        uc = (cw_ref[2:3, :] * u4 + cb_ref[...] + cw_ref[0:1, :] * shifted(2)
              + cw_ref[1:2, :] * shifted(1) + cw_ref[3:4, :] * shifted(-1)).reshape(tb, w)
        a, x = coeffs(uc)
        ac_scr[...] = a
        hl_scr[...] = x
    else:
        row = lax.broadcasted_iota(jnp.int32, (tb, w), 0)
        uc = cw_ref[2:3, :] * u + cb_ref[...]
        uc = uc + cw_ref[0:1, :] * jnp.where(row >= 2, pltpu.roll(u, 2, 0), 0.0)
        uc = uc + cw_ref[1:2, :] * jnp.where(row >= 1, pltpu.roll(u, 1, 0), 0.0)
        uc = uc + cw_ref[3:4, :] * jnp.where(row < tb - 1, pltpu.roll(u, tb - 1, 0), 0.0)
        a, x = coeffs(uc)
        interleave(a, a_scr, ac_scr)
        interleave(x, x_scr, hl_scr)

    def pass1(ii, carry):
        i = (seg - 1 - ii) if reverse else ii
        h, cum = carry
        rows = pl.ds(pl.multiple_of(i * RG_SEGS, RG_SEGS), RG_SEGS)
        ai = ac_scr[rows, :]
        h = ai * h + hl_scr[rows, :]
        cum = ai * cum
        hl_scr[rows, :] = h
        ac_scr[rows, :] = cum
        return h, cum

    h_end, a_tot = lax.fori_loop(
        0, seg, pass1, (jnp.zeros((RG_SEGS, w), F32), jnp.ones((RG_SEGS, w), F32)), unroll=4)

    cur = h_ref[...]
    per_seg = [None] * RG_SEGS
    for sgm in (range(RG_SEGS - 1, -1, -1) if reverse else range(RG_SEGS)):
        per_seg[sgm] = cur
        cur = h_end[sgm:sgm + 1, :] + a_tot[sgm:sgm + 1, :] * cur
    enter = jnp.concatenate(per_seg, axis=0)
    h_out = cur
    h_ref[...] = h_out

    def pass2(i, carry):
        rows = pl.ds(pl.multiple_of(i * RG_SEGS, RG_SEGS), RG_SEGS)
        hfull = hl_scr[rows, :] + ac_scr[rows, :] * enter
        for c in range(nsl):
            hs_scr[c, pl.ds(i, RG_SEGS, stride=pitch), :] = hfull[:, c * LANES:(c + 1) * LANES]
        return carry

    lax.fori_loop(0, seg, pass2, 0, unroll=4)

    for sgm in range(RG_SEGS):
        rows = slice(sgm * seg, (sgm + 1) * seg)
        hn = jnp.concatenate([hs_scr[c, sgm * pitch:sgm * pitch + seg, :] for c in range(nsl)],
                             axis=1)
        o_ref[0, rows, :] = hn

    @pl.when(j == pl.num_programs(1) - 1)
    def _():
        hout_ref[0] = h_out


def _rg_scan(p, conv_w, conv_b, wg, bg, ls, h0, reverse, row_conv, tb):
    b, t, _ = p.shape
    nb = t // tb
    w = RG_WIDTH

    def blk(j):
        return (nb - 1 - j) if reverse else j

    def col(cidx):
        return pl.BlockSpec((1, tb, w), lambda i, j: (i, blk(j), cidx))

    h_spec = pl.BlockSpec((1, 1, w), lambda i, j: (i, 0, 0))
    in_specs = [col(COL_U), _const_spec((4, w)), _const_spec((1, w)),
                _const_spec((w, 2 * w)), _const_spec((1, 2 * w)), _const_spec((1, w)), h_spec]
    args = [p, conv_w, conv_b.reshape(1, w), wg, bg.reshape(1, 2 * w), ls.reshape(1, w), h0]
    return pl.pallas_call(
        functools.partial(_rg_kernel, reverse=reverse, row_conv=row_conv, tb=tb),
        grid=(b, nb),
        in_specs=in_specs,
        out_specs=[pl.BlockSpec((1, tb, w), lambda i, j: (i, blk(j), 0)), h_spec],
        out_shape=[jax.ShapeDtypeStruct((b, t, w), F32), jax.ShapeDtypeStruct((b, 1, w), F32)],
        scratch_shapes=[pltpu.VMEM((w // LANES, tb + RG_SEGS * SUBLANES, LANES), F32)] * 3
        + [pltpu.VMEM((tb, w), F32)] * 2 + [pltpu.VMEM((1, w), F32)]
        + [pltpu.VMEM((tb + 2 * SUBLANES, w), F32)],
        compiler_params=_params(("parallel", "arbitrary")),
        name="rglru_bwd" if reverse else "rglru_fwd",
    )(*args)


def _out_ffn_kernel(x_ref, os_ref, og_ref, hf_ref, hb_ref, gate_ref, m_ref, g_ref, gn_ref,
                    wo_ref, wg_ref, wu_ref, wd_ref, o_ref):
    osum = os_ref[0]
    heads = [_rms(osum[:, h * HG_DK:(h + 1) * HG_DK], gn_ref[...]) for h in range(HG_HEADS)]
    hg = (jnp.concatenate(heads, axis=1) * _silu(og_ref[0])).astype(BF16)
    rg = ((hf_ref[0] + hb_ref[0]) * _gelu_tanh(gate_ref[0])).astype(BF16)
    o = jnp.dot(hg, wo_ref[:HG_WIDTH, :], preferred_element_type=F32)
    o = o + jnp.dot(rg, wo_ref[HG_WIDTH:, :], preferred_element_type=F32)
    x1 = x_ref[0] + m_ref[0, 2:3, :] * _rms(o, g_ref[1:2, :])
    f = (_rms(x1, g_ref[2:3, :]) * (1.0 + m_ref[0, 4:5, :]) + m_ref[0, 3:4, :]).astype(BF16)
    gate = jnp.dot(f, wg_ref[...], preferred_element_type=F32)
    up = jnp.dot(f, wu_ref[...], preferred_element_type=F32)
    ff = jnp.dot((_silu(gate) * up).astype(BF16), wd_ref[...], preferred_element_type=F32)
    o_ref[0] = x1 + m_ref[0, 5:6, :] * _rms(ff, g_ref[3:4, :])


def _out_ffn(x, osum, p, hf, hb, m, g, gnorm, wo, wg, wu, wd, tm):
    b, t, d = x.shape
    mb = m.shape[0]
    dff = wg.shape[1]
    mixw = wo.shape[0]

    def resident(shape):
        return pl.BlockSpec(shape, lambda i, j: (0, 0), pipeline_mode=pl.Buffered(1))

    def tok(width):
        return pl.BlockSpec((1, tm, width), lambda i, j: (i, j, 0))

    return pl.pallas_call(
        _out_ffn_kernel,
        grid=(b, t // tm),
        in_specs=[tok(d), tok(HG_WIDTH),
                  pl.BlockSpec((1, tm, HG_WIDTH), lambda i, j: (i, j, COL_G)),
                  tok(RG_WIDTH), tok(RG_WIDTH),
                  pl.BlockSpec((1, tm, RG_WIDTH), lambda i, j: (i, j, COL_GATE)),
                  pl.BlockSpec((1, N_MOD, d), lambda i, j: (i if mb > 1 else 0, 0, 0)),
                  _const_spec((4, d)), _const_spec((1, HG_DK)),
                  resident((mixw, d)), resident((d, dff)), resident((d, dff)),
                  resident((dff, d))],
        out_specs=tok(d),
        out_shape=jax.ShapeDtypeStruct((b, t, d), F32),
        compiler_params=_params(("parallel", "parallel")),
        name="out_ffn",
    )(x, osum, p, hf, hb, p, m, g, gnorm.reshape(1, HG_DK), wo, wg, wu, wd)


def _block_diag(wb):
    nb, n, _ = wb.shape
    eye = jnp.eye(nb, dtype=wb.dtype)
    return jnp.einsum("nij,nm->nimj", wb, eye).reshape(nb * n, nb * n)


def _largest_tile(t, cap, step):
    best = step
    for cand in range(step, min(t, cap) + 1, step):
        if t % cand == 0:
            best = cand
    return best


def kernel(x, c, ctx, c_ctx, w_mod, b_mod, norm_g, w_in, hg_lb_logits, hg_gnorm, rg_conv_w, rg_conv_b, rg_w_a, rg_b_a, rg_w_x, rg_b_x, rg_lambda, w_out, w_ffn_gate, w_ffn_up, w_ffn_down):
    depth = w_mod.shape[0]
    bsz, seq, d = x.shape
    n_ctx = ctx.shape[1]

    lb_all = jnp.cumsum(jax.nn.softmax(hg_lb_logits.astype(F32), axis=0), axis=0)
    lb_all = lb_all - lb_all[0]
    log_sig = jax.nn.log_sigmoid(rg_lambda.astype(F32))

    rows = -(-(bsz + 1) // SUBLANES) * SUBLANES
    cvec = jnp.zeros((rows, d), F32).at[:bsz].set(c).at[bsz].set(c_ctx)
    mod = _modulation(cvec, w_mod, b_mod).reshape(depth, rows, N_MOD, d)

    tm_lat = _largest_tile(seq, 512, 128)
    tm_ctx = _largest_tile(n_ctx, 512, 128)
    tb_hg = _largest_tile(seq, 1024, CHUNK)
    tb_rg = _largest_tile(seq, RG_SEGS * GRID_W, RG_SEGS * GRID_W)

    x_lat, x_ctx = x, ctx
    for layer in range(depth):
        with_ctx = layer < depth - 1
        m_l = mod[layer, :bsz]
        m_c = mod[layer, bsz:bsz + 1]
        g = norm_g[layer]
        w_in_bf = w_in[layer].astype(BF16)
        p_l = _in_proj(x_lat, m_l, g[0], w_in_bf, tm_lat)
        p_c = _in_proj(x_ctx, m_c, g[0], w_in_bf, tm_ctx)

        s0 = jnp.zeros((bsz, HG_HEADS, HG_DK, HG_DK), F32)
        of_c, s_f = _hg_scan(p_c, lb_all[layer, 0], s0, False, n_ctx)
        of_l, _ = _hg_scan(p_l, lb_all[layer, 0], s_f, False, tb_hg)
        os_c, s_b = _hg_scan(p_c, lb_all[layer, 1], s0, True, n_ctx, of=of_c)
        os_l, _ = _hg_scan(p_l, lb_all[layer, 1], s_b, True, tb_hg, of=of_l)

        h0 = jnp.zeros((bsz, 1, RG_WIDTH), F32)
        cw = rg_conv_w[layer].astype(F32)
        cb = rg_conv_b[layer].astype(F32)
        wgs = [(0.5 * jnp.concatenate([_block_diag(rg_w_a[layer, dd]),
                                       _block_diag(rg_w_x[layer, dd])], axis=1)).astype(BF16)
               for dd in range(2)]
        bgs = [0.5 * jnp.concatenate([rg_b_a[layer, dd], rg_b_x[layer, dd]]).astype(F32)
               for dd in range(2)]
        hf_c, h_f = _rg_scan(p_c, cw, cb, wgs[0], bgs[0], log_sig[layer, 0], h0, False, False,
                             n_ctx)
        hf_l, _ = _rg_scan(p_l, cw, cb, wgs[0], bgs[0], log_sig[layer, 0], h_f, False, True,
                           tb_rg)
        hb_c, h_b = _rg_scan(p_c, cw, cb, wgs[1], bgs[1], log_sig[layer, 1], h0, True, False,
                             n_ctx)
        hb_l, _ = _rg_scan(p_l, cw, cb, wgs[1], bgs[1], log_sig[layer, 1], h_b, True, True,
                           tb_rg)

        wo = w_out[layer].astype(BF16)
        wg_ = w_ffn_gate[layer].astype(BF16)
        wu_ = w_ffn_up[layer].astype(BF16)
        wd_ = w_ffn_down[layer].astype(BF16)
        gn = hg_gnorm[layer].astype(F32)
        x_lat = _out_ffn(x_lat, os_l, p_l, hf_l, hb_l, m_l, g, gn, wo, wg_, wu_, wd_, tm_lat)
        if with_ctx:
            x_ctx = _out_ffn(x_ctx, os_c, p_c, hf_c, hb_c, m_c, g, gn, wo, wg_, wu_, wd_,
                             tm_ctx)
    return x_lat
```

```python
import functools

import numpy as np
import jax
import jax.numpy as jnp
from jax import lax
from jax.experimental import pallas as pl
from jax.experimental.pallas import tpu as pltpu

F32 = jnp.float32
BF16 = jnp.bfloat16

EPS = 1e-6
N_MOD = 6
HG_HEADS = 4
HG_DK = 128
HG_WIDTH = HG_HEADS * HG_DK
RG_WIDTH = 512
RG_BLOCKS = 8
RG_C = 8.0
GRID_W = 64
CHUNK = 128
LEVELS = (1, 2, 4, 8, 16, 32, 64)
SUBLANES = 8
LANES = 128
RG_SEGS = 16
LOG2E = 1.4426950408889634
VMEM_LIMIT = 56 * 1024 * 1024

COL_Q, COL_ZF, COL_ZB, COL_V, COL_G, COL_U, COL_GATE = range(7)


def _params(sem):
    return pltpu.CompilerParams(dimension_semantics=sem, vmem_limit_bytes=VMEM_LIMIT)


def _const_spec(shape):
    nd = len(shape)
    return pl.BlockSpec(shape, lambda *_: (0,) * nd)


def _sigmoid(x):
    return 0.5 * jnp.tanh(0.5 * x) + 0.5


def _silu(x):
    h = 0.5 * x
    return h + h * jnp.tanh(h)


def _rms(x, g):
    return x * lax.rsqrt(jnp.mean(x * x, axis=-1, keepdims=True) + EPS) * g


def _mod_kernel(c_ref, w_ref, b_ref, o_ref):
    a = _silu(c_ref[...])
    o_ref[0] = jnp.dot(a, w_ref[0], precision=lax.Precision.HIGHEST,
                       preferred_element_type=F32) + b_ref[0]


def _modulation(cvec, w_mod, b_mod):
    depth, d, n = w_mod.shape
    rows = cvec.shape[0]
    nb = 4
    bn = n // nb
    return pl.pallas_call(
        _mod_kernel,
        grid=(depth, nb),
        in_specs=[
            pl.BlockSpec((rows, d), lambda l, j: (0, 0)),
            pl.BlockSpec((1, d, bn), lambda l, j: (l, 0, j)),
            pl.BlockSpec((1, 1, bn), lambda l, j: (l, 0, j)),
        ],
        out_specs=pl.BlockSpec((1, rows, bn), lambda l, j: (l, 0, j)),
        out_shape=jax.ShapeDtypeStruct((depth, rows, n), F32),
        compiler_params=_params(("parallel", "parallel")),
        name="adaln_mod",
    )(cvec, w_mod, b_mod.reshape(depth, 1, n))


def _proj_kernel(x_ref, m_ref, g_ref, w_ref, o_ref):
    h = _rms(x_ref[0], g_ref[...]) * (1.0 + m_ref[0, 1:2, :]) + m_ref[0, 0:1, :]
    o_ref[0] = jnp.dot(h.astype(BF16), w_ref[...].astype(BF16), preferred_element_type=F32)


def _in_proj(x, m, g0, w_in_bf, tm):
    b, t, d = x.shape
    n = w_in_bf.shape[1]
    mb = m.shape[0]
    return pl.pallas_call(
        _proj_kernel,
        grid=(b, t // tm),
        in_specs=[
            pl.BlockSpec((1, tm, d), lambda i, j: (i, j, 0)),
            pl.BlockSpec((1, N_MOD, d), lambda i, j: (i if mb > 1 else 0, 0, 0)),
            _const_spec((1, d)),
            pl.BlockSpec((d, n), lambda i, j: (0, 0), pipeline_mode=pl.Buffered(1)),
        ],
        out_specs=pl.BlockSpec((1, tm, n), lambda i, j: (i, j, 0)),
        out_shape=jax.ShapeDtypeStruct((b, t, n), F32),
        compiler_params=_params(("parallel", "parallel")),
        name="in_proj",
    )(x, m, g0.reshape(1, d), w_in_bf)


def _level_masks(reverse):
    t = np.arange(CHUNK)[:, None]
    s = np.arange(CHUNK)[None, :]
    if reverse:
        t, s = s, t
    masks = [t == s]
    for b in LEVELS:
        masks.append((t // b == s // b + 1) & ((s // b) % 2 == 0))
    return np.stack(masks).astype(np.float32)


def _level_signs(reverse):
    sub = np.arange(SUBLANES)[:, None]
    out = []
    for b in (2, 4):
        later = ((sub & b) == 0) if reverse else ((sub & b) != 0)
        out.append(np.broadcast_to(np.where(later, 1.0, -1.0), (SUBLANES, HG_WIDTH)))
    return np.stack(out).astype(np.float32)


def _hg_chunk(q_raw, z, v, lb, st_ref, mask_ref, sign_ref, reverse):
    n = CHUNK
    w = q_raw.shape[1]
    ng = n // SUBLANES
    q = _silu(q_raw)
    e = jnp.exp(-jnp.abs(z))
    r = 1.0 / (1.0 + e)
    pos = z >= 0
    sig = jnp.where(pos, r, e * r)
    sig_neg = jnp.where(pos, e * r, r)
    k = (1.0 - lb) * sig_neg
    f = lb + (1.0 - lb) * sig
    lf2 = jnp.log(f) * LOG2E

    sub = lax.broadcasted_iota(jnp.int32, (ng, SUBLANES, w), 1)

    def later(b):
        on = (sub & b) != 0
        return jnp.logical_not(on) if reverse else on

    g3 = lf2.reshape(ng, SUBLANES, w)
    pos_in_group = (SUBLANES - 1 - sub) if reverse else sub
    d = 1
    while d < SUBLANES:
        sh = pltpu.roll(g3, (SUBLANES - d) if reverse else d, 1)
        g3 = g3 + jnp.where(pos_in_group >= d, sh, 0.0)
        d *= 2
    edge = 0 if reverse else SUBLANES - 1
    groups = [None] * ng
    carry = None
    for g in (range(ng - 1, -1, -1) if reverse else range(ng)):
        blk = g3[g] if carry is None else g3[g] + carry
        groups[g] = blk
        carry = blk[edge:edge + 1, :]
    lc = jnp.concatenate(groups, axis=0)
    lc3 = lc.reshape(ng, SUBLANES, w)
    q3 = q.reshape(ng, SUBLANES, w)
    k3 = k.reshape(ng, SUBLANES, w)

    acc = [None] * HG_HEADS
    vb = v.astype(BF16)

    def add_scores(lvl, a_bf, b_bf):
        for h in range(HG_HEADS):
            sl = slice(h * HG_DK, (h + 1) * HG_DK)
            sc = lax.dot_general(a_bf[:, sl], b_bf[:, sl], (((1,), (1,)), ((), ())),
                                 preferred_element_type=F32).astype(BF16) * mask_ref[lvl]
            acc[h] = sc if acc[h] is None else acc[h] + sc

    def bcast_row(r):
        return jnp.broadcast_to(lc3[:, r:r + 1, :], (ng, SUBLANES, w))

    add_scores(0, q.astype(BF16), k.astype(BF16))
    for lvl, b in enumerate(LEVELS):
        if b == 1:
            xb = jnp.where(later(1), q3 * f.reshape(ng, SUBLANES, w), k3)
        elif b < SUBLANES:
            rows = [sb * 2 * b + (b if reverse else b - 1) for sb in range(SUBLANES // (2 * b))]
            anchor = bcast_row(rows[0])
            for i in range(1, len(rows)):
                anchor = jnp.where(sub >= i * 2 * b, bcast_row(rows[i]), anchor)
            wgt = jnp.exp2((lc3 - anchor) * sign_ref[lvl - 1])
            xb = jnp.where(later(b), q3, k3) * wgt
        else:
            pieces = []
            for sb in range(n // (2 * b)):
                lo, mid, hi = sb * 2 * b, sb * 2 * b + b, (sb + 1) * 2 * b
                if reverse:
                    anc = lc[mid:mid + 1, :]
                    pieces.append(q[lo:mid] * jnp.exp2(lc[lo:mid] - anc))
                    pieces.append(k[mid:hi] * jnp.exp2(anc - lc[mid:hi]))
                else:
                    anc = lc[mid - 1:mid, :]
                    pieces.append(k[lo:mid] * jnp.exp2(anc - lc[lo:mid]))
                    pieces.append(q[mid:hi] * jnp.exp2(lc[mid:hi] - anc))
            xb = jnp.concatenate(pieces, axis=0)
        xb = xb.reshape(n, w).astype(BF16)
        add_scores(lvl + 1, xb, xb)

    last = lc[0:1, :] if reverse else lc[n - 1:n, :]
    q_in = (q * jnp.exp2(lc)).astype(BF16)
    k_out = (k * jnp.exp2(last - lc)).astype(BF16)
    total = jnp.exp2(last)
    outs = []
    for h in range(HG_HEADS):
        sl = slice(h * HG_DK, (h + 1) * HG_DK)
        st = st_ref[h]
        o = jnp.dot(acc[h], vb[:, sl], preferred_element_type=F32)
        o = o + lax.dot_general(q_in[:, sl], st.astype(BF16), (((1,), (1,)), ((), ())),
                                preferred_element_type=F32)
        st_ref[h] = st * total[:, sl] + lax.dot_general(
            vb[:, sl], k_out[:, sl], (((0,), (0,)), ((), ())), preferred_element_type=F32)
        outs.append(o)
    return jnp.concatenate(outs, axis=1)


def _hg_kernel(*refs, reverse, readout, n_chunks):
    if readout:
        (q_ref, z_ref, v_ref, lb_ref, s0_ref, mask_ref, sign_ref, of_ref,
         o_ref, sout_ref, st_ref) = refs
    else:
        (q_ref, z_ref, v_ref, lb_ref, s0_ref, mask_ref, sign_ref,
         o_ref, sout_ref, st_ref) = refs
    j = pl.program_id(1)

    @pl.when(j == 0)
    def _():
        st_ref[...] = s0_ref[0]

    lb = lb_ref[...]

    def body(ci, carry):
        c = (n_chunks - 1 - ci) if reverse else ci
        rows = pl.ds(pl.multiple_of(c * CHUNK, CHUNK), CHUNK)
        o = _hg_chunk(q_ref[0, rows, :], z_ref[0, rows, :], v_ref[0, rows, :], lb,
                      st_ref, mask_ref, sign_ref, reverse)
        if readout:
            o = o + of_ref[0, rows, :]
        o_ref[0, rows, :] = o
        return carry

    lax.fori_loop(0, n_chunks, body, 0, unroll=4)

    @pl.when(j == pl.num_programs(1) - 1)
    def _():
        sout_ref[0] = st_ref[...]


def _hg_scan(p, lb, s0, reverse, tb, of=None):
    b, t, _ = p.shape
    nb = t // tb
    readout = of is not None
    masks = jnp.asarray(_level_masks(reverse), dtype=BF16)

    def blk(j):
        return (nb - 1 - j) if reverse else j

    def col(cidx):
        return pl.BlockSpec((1, tb, HG_WIDTH), lambda i, j: (i, blk(j), cidx))

    state_spec = pl.BlockSpec((1, HG_HEADS, HG_DK, HG_DK), lambda i, j: (i, 0, 0, 0))
    in_specs = [col(COL_Q), col(COL_ZB if reverse else COL_ZF), col(COL_V),
                _const_spec((1, HG_WIDTH)), state_spec,
                _const_spec((len(LEVELS) + 1, CHUNK, CHUNK)),
                _const_spec((2, SUBLANES, HG_WIDTH))]
    args = [p, p, p, lb.reshape(1, HG_WIDTH), s0, masks, jnp.asarray(_level_signs(reverse))]
    if readout:
        in_specs += [pl.BlockSpec((1, tb, HG_WIDTH), lambda i, j: (i, blk(j), 0))]
        args += [of]
    return pl.pallas_call(
        functools.partial(_hg_kernel, reverse=reverse, readout=readout,
                          n_chunks=tb // CHUNK),
        grid=(b, nb),
        in_specs=in_specs,
        out_specs=[pl.BlockSpec((1, tb, HG_WIDTH), lambda i, j: (i, blk(j), 0)), state_spec],
        out_shape=[jax.ShapeDtypeStruct((b, t, HG_WIDTH), F32),
                   jax.ShapeDtypeStruct((b, HG_HEADS, HG_DK, HG_DK), F32)],
        scratch_shapes=[pltpu.VMEM((HG_HEADS, HG_DK, HG_DK), F32)],
        compiler_params=_params(("parallel", "arbitrary")),
        name="hgrn2_bwd" if reverse else "hgrn2_fwd",
    )(*args)


def _gelu_tanh(x):
    return 0.5 * x * (1.0 + jnp.tanh(0.7978845608028654 * (x + 0.044715 * (x * x * x))))


def _rg_kernel(*refs, reverse, row_conv, tb):
    (u_ref, cw_ref, cb_ref, wg_ref, bg_ref, ls_ref, h0_ref,
     o_ref, hout_ref, a_scr, x_scr, hs_scr, hl_scr, ac_scr, h_ref, up_scr) = refs
    j = pl.program_id(1)

    @pl.when(j == 0)
    def _():
        h_ref[...] = h0_ref[0]

    u = u_ref[0]
    w = u.shape[1]
    seg = tb // RG_SEGS
    pitch = seg + SUBLANES
    nsl = w // LANES

    def interleave(val, stage_scr, dst_scr):
        for c in range(nsl):
            for sgm in range(RG_SEGS):
                stage_scr[c, sgm * pitch:sgm * pitch + seg, :] = val[sgm * seg:(sgm + 1) * seg,
                                                                     c * LANES:(c + 1) * LANES]

        def body(i, carry):
            rows = pl.ds(pl.multiple_of(i * RG_SEGS, RG_SEGS), RG_SEGS)
            for c in range(nsl):
                dst_scr[rows, c * LANES:(c + 1) * LANES] = stage_scr[
                    c, pl.ds(i, RG_SEGS, stride=pitch), :]
            return carry

        lax.fori_loop(0, seg, body, 0, unroll=8)

    def coeffs(uc):
        gates = jnp.dot(uc.astype(BF16), wg_ref[...], preferred_element_type=F32) + bg_ref[...]
        r = 0.5 * jnp.tanh(gates[:, :w]) + 0.5
        i = 0.5 * jnp.tanh(gates[:, w:]) + 0.5
        a = jnp.exp2(r * ((RG_C * LOG2E) * ls_ref[...]))
        y = 1.0 - a * a
        return a, jnp.where(y > 0.0, y * lax.rsqrt(y), 0.0) * (i * uc)

    if row_conv:
        interleave(u, a_scr, up_scr)
        n_rows = seg // GRID_W
        u4 = up_scr[0:tb, :].reshape(n_rows, GRID_W, RG_SEGS, w)

        def shifted(d):
            pad = jnp.zeros((n_rows, abs(d), RG_SEGS, w), F32)
            if d > 0:
                return jnp.concatenate([pad, u4[:, :GRID_W - d]], axis=1)
            return jnp.concatenate([u4[:, -d:], pad], axis=1)

        uc = (cw_ref[2:3, :] * u4 + cb_ref[...] + cw_ref[0:1, :] * shifted(2)
              + cw_ref[1:2, :] * shifted(1) + cw_ref[3:4, :] * shifted(-1)).reshape(tb, w)
        a, x = coeffs(uc)
        ac_scr[...] = a
        hl_scr[...] = x
    else:
        row = lax.broadcasted_iota(jnp.int32, (tb, w), 0)
        uc = cw_ref[2:3, :] * u + cb_ref[...]
        uc = uc + cw_ref[0:1, :] * jnp.where(row >= 2, pltpu.roll(u, 2, 0), 0.0)
        uc = uc + cw_ref[1:2, :] * jnp.where(row >= 1, pltpu.roll(u, 1, 0), 0.0)
        uc = uc + cw_ref[3:4, :] * jnp.where(row < tb - 1, pltpu.roll(u, tb - 1, 0), 0.0)
        a, x = coeffs(uc)
        interleave(a, a_scr, ac_scr)
        interleave(x, x_scr, hl_scr)

    def pass1(ii, carry):
        i = (seg - 1 - ii) if reverse else ii
        h, cum = carry
        rows = pl.ds(pl.multiple_of(i * RG_SEGS, RG_SEGS), RG_SEGS)
        ai = ac_scr[rows, :]
        h = ai * h + hl_scr[rows, :]
        cum = ai * cum
        hl_scr[rows, :] = h
        ac_scr[rows, :] = cum
        return h, cum

    h_end, a_tot = lax.fori_loop(
        0, seg, pass1, (jnp.zeros((RG_SEGS, w), F32), jnp.ones((RG_SEGS, w), F32)), unroll=4)

    cur = h_ref[...]
    per_seg = [None] * RG_SEGS
    for sgm in (range(RG_SEGS - 1, -1, -1) if reverse else range(RG_SEGS)):
        per_seg[sgm] = cur
        cur = h_end[sgm:sgm + 1, :] + a_tot[sgm:sgm + 1, :] * cur
    enter = jnp.concatenate(per_seg, axis=0)
    h_out = cur
    h_ref[...] = h_out

    def pass2(i, carry):
        rows = pl.ds(pl.multiple_of(i * RG_SEGS, RG_SEGS), RG_SEGS)
        hfull = hl_scr[rows, :] + ac_scr[rows, :] * enter
        for c in range(nsl):
            hs_scr[c, pl.ds(i, RG_SEGS, stride=pitch), :] = hfull[:, c * LANES:(c + 1) * LANES]
        return carry

    lax.fori_loop(0, seg, pass2, 0, unroll=4)

    for sgm in range(RG_SEGS):
        rows = slice(sgm * seg, (sgm + 1) * seg)
        hn = jnp.concatenate([hs_scr[c, sgm * pitch:sgm * pitch + seg, :] for c in range(nsl)],
                             axis=1)
        o_ref[0, rows, :] = hn

    @pl.when(j == pl.num_programs(1) - 1)
    def _():
        hout_ref[0] = h_out


def _rg_scan(p, conv_w, conv_b, wg, bg, ls, h0, reverse, row_conv, tb):
    b, t, _ = p.shape
    nb = t // tb
    w = RG_WIDTH

    def blk(j):
        return (nb - 1 - j) if reverse else j

    def col(cidx):
        return pl.BlockSpec((1, tb, w), lambda i, j: (i, blk(j), cidx))

    h_spec = pl.BlockSpec((1, 1, w), lambda i, j: (i, 0, 0))
    in_specs = [col(COL_U), _const_spec((4, w)), _const_spec((1, w)),
                _const_spec((w, 2 * w)), _const_spec((1, 2 * w)), _const_spec((1, w)), h_spec]
    args = [p, conv_w, conv_b.reshape(1, w), wg, bg.reshape(1, 2 * w), ls.reshape(1, w), h0]
    return pl.pallas_call(
        functools.partial(_rg_kernel, reverse=reverse, row_conv=row_conv, tb=tb),
        grid=(b, nb),
        in_specs=in_specs,
        out_specs=[pl.BlockSpec((1, tb, w), lambda i, j: (i, blk(j), 0)), h_spec],
        out_shape=[jax.ShapeDtypeStruct((b, t, w), F32), jax.ShapeDtypeStruct((b, 1, w), F32)],
        scratch_shapes=[pltpu.VMEM((w // LANES, tb + RG_SEGS * SUBLANES, LANES), F32)] * 3
        + [pltpu.VMEM((tb, w), F32)] * 2 + [pltpu.VMEM((1, w), F32)]
        + [pltpu.VMEM((tb + 2 * SUBLANES, w), F32)],
        compiler_params=_params(("parallel", "arbitrary")),
        name="rglru_bwd" if reverse else "rglru_fwd",
    )(*args)


def _out_ffn_kernel(x_ref, os_ref, og_ref, hf_ref, hb_ref, gate_ref, m_ref, g_ref, gn_ref,
                    wo_ref, wg_ref, wu_ref, wd_ref, o_ref):
    osum = os_ref[0]
    heads = [_rms(osum[:, h * HG_DK:(h + 1) * HG_DK], gn_ref[...]) for h in range(HG_HEADS)]
    hg = (jnp.concatenate(heads, axis=1) * _silu(og_ref[0])).astype(BF16)
    rg = ((hf_ref[0] + hb_ref[0]) * _gelu_tanh(gate_ref[0])).astype(BF16)
    o = jnp.dot(hg, wo_ref[:HG_WIDTH, :], preferred_element_type=F32)
    o = o + jnp.dot(rg, wo_ref[HG_WIDTH:, :], preferred_element_type=F32)
    x1 = x_ref[0] + m_ref[0, 2:3, :] * _rms(o, g_ref[1:2, :])
    f = (_rms(x1, g_ref[2:3, :]) * (1.0 + m_ref[0, 4:5, :]) + m_ref[0, 3:4, :]).astype(BF16)
    gate = jnp.dot(f, wg_ref[...], preferred_element_type=F32)
    up = jnp.dot(f, wu_ref[...], preferred_element_type=F32)
    ff = jnp.dot((_silu(gate) * up).astype(BF16), wd_ref[...], preferred_element_type=F32)
    o_ref[0] = x1 + m_ref[0, 5:6, :] * _rms(ff, g_ref[3:4, :])


def _out_ffn(x, osum, p, hf, hb, m, g, gnorm, wo, wg, wu, wd, tm):
    b, t, d = x.shape
    mb = m.shape[0]
    dff = wg.shape[1]
    mixw = wo.shape[0]

    def resident(shape):
        return pl.BlockSpec(shape, lambda i, j: (0, 0), pipeline_mode=pl.Buffered(1))

    def tok(width):
        return pl.BlockSpec((1, tm, width), lambda i, j: (i, j, 0))

    return pl.pallas_call(
        _out_ffn_kernel,
        grid=(b, t // tm),
        in_specs=[tok(d), tok(HG_WIDTH),
                  pl.BlockSpec((1, tm, HG_WIDTH), lambda i, j: (i, j, COL_G)),
                  tok(RG_WIDTH), tok(RG_WIDTH),
                  pl.BlockSpec((1, tm, RG_WIDTH), lambda i, j: (i, j, COL_GATE)),
                  pl.BlockSpec((1, N_MOD, d), lambda i, j: (i if mb > 1 else 0, 0, 0)),
                  _const_spec((4, d)), _const_spec((1, HG_DK)),
                  resident((mixw, d)), resident((d, dff)), resident((d, dff)),
                  resident((dff, d))],
        out_specs=tok(d),
        out_shape=jax.ShapeDtypeStruct((b, t, d), F32),
        compiler_params=_params(("parallel", "parallel")),
        name="out_ffn",
    )(x, osum, p, hf, hb, p, m, g, gnorm.reshape(1, HG_DK), wo, wg, wu, wd)


def _block_diag(wb):
    nb, n, _ = wb.shape
    eye = jnp.eye(nb, dtype=wb.dtype)
    return jnp.einsum("nij,nm->nimj", wb, eye).reshape(nb * n, nb * n)


def _largest_tile(t, cap, step):
    best = step
    for cand in range(step, min(t, cap) + 1, step):
        if t % cand == 0:
            best = cand
    return best


def kernel(x, c, ctx, c_ctx, w_mod, b_mod, norm_g, w_in, hg_lb_logits, hg_gnorm, rg_conv_w, rg_conv_b, rg_w_a, rg_b_a, rg_w_x, rg_b_x, rg_lambda, w_out, w_ffn_gate, w_ffn_up, w_ffn_down):
    depth = w_mod.shape[0]
    bsz, seq, d = x.shape
    n_ctx = ctx.shape[1]

    lb_all = jnp.cumsum(jax.nn.softmax(hg_lb_logits.astype(F32), axis=0), axis=0)
    lb_all = lb_all - lb_all[0]
    log_sig = jax.nn.log_sigmoid(rg_lambda.astype(F32))

    rows = -(-(bsz + 1) // SUBLANES) * SUBLANES
    cvec = jnp.zeros((rows, d), F32).at[:bsz].set(c).at[bsz].set(c_ctx)
    mod = _modulation(cvec, w_mod, b_mod).reshape(depth, rows, N_MOD, d)

    tm_lat = _largest_tile(seq, 512, 128)
    tm_ctx = _largest_tile(n_ctx, 512, 128)
    tb_hg = _largest_tile(seq, 1024, CHUNK)
    tb_rg = _largest_tile(seq, RG_SEGS * GRID_W, RG_SEGS * GRID_W)

    x_lat, x_ctx = x, ctx
    for layer in range(depth):
        with_ctx = layer < depth - 1
        m_l = mod[layer, :bsz]
        m_c = mod[layer, bsz:bsz + 1]
        g = norm_g[layer]
        w_in_bf = w_in[layer]
        p_l = _in_proj(x_lat, m_l, g[0], w_in_bf, tm_lat)
        p_c = _in_proj(x_ctx, m_c, g[0], w_in_bf, tm_ctx)

        s0 = jnp.zeros((bsz, HG_HEADS, HG_DK, HG_DK), F32)
        of_c, s_f = _hg_scan(p_c, lb_all[layer, 0], s0, False, n_ctx)
        of_l, _ = _hg_scan(p_l, lb_all[layer, 0], s_f, False, tb_hg)
        os_c, s_b = _hg_scan(p_c, lb_all[layer, 1], s0, True, n_ctx, of=of_c)
        os_l, _ = _hg_scan(p_l, lb_all[layer, 1], s_b, True, tb_hg, of=of_l)

        h0 = jnp.zeros((bsz, 1, RG_WIDTH), F32)
        cw = rg_conv_w[layer].astype(F32)
        cb = rg_conv_b[layer].astype(F32)
        wgs = [(0.5 * jnp.concatenate([_block_diag(rg_w_a[layer, dd]),
                                       _block_diag(rg_w_x[layer, dd])], axis=1)).astype(BF16)
               for dd in range(2)]
        bgs = [0.5 * jnp.concatenate([rg_b_a[layer, dd], rg_b_x[layer, dd]]).astype(F32)
               for dd in range(2)]
        hf_c, h_f = _rg_scan(p_c, cw, cb, wgs[0], bgs[0], log_sig[layer, 0], h0, False, False,
                             n_ctx)
        hf_l, _ = _rg_scan(p_l, cw, cb, wgs[0], bgs[0], log_sig[layer, 0], h_f, False, True,
                           tb_rg)
        hb_c, h_b = _rg_scan(p_c, cw, cb, wgs[1], bgs[1], log_sig[layer, 1], h0, True, False,
                             n_ctx)
        hb_l, _ = _rg_scan(p_l, cw, cb, wgs[1], bgs[1], log_sig[layer, 1], h_b, True, True,
                           tb_rg)

        wo = w_out[layer].astype(BF16)
        wg_ = w_ffn_gate[layer].astype(BF16)
        wu_ = w_ffn_up[layer].astype(BF16)
        wd_ = w_ffn_down[layer].astype(BF16)
        gn = hg_gnorm[layer].astype(F32)
        x_lat = _out_ffn(x_lat, os_l, p_l, hf_l, hb_l, m_l, g, gn, wo, wg_, wu_, wd_, tm_lat)
        if with_ctx:
            x_ctx = _out_ffn(x_ctx, os_c, p_c, hf_c, hb_c, m_c, g, gn, wo, wg_, wu_, wd_,
                             tm_ctx)
    return x_lat
```
